```python
import math
import jax, jax.numpy as jnp
from jax import lax
import numpy as np

D_MODEL = 1024
BATCH = 16
SEQ = 2048
DEPTH = 2

RET_HEADS = 4
RET_HEAD_DIM = 128
ATT_HEADS = 4
ATT_HEAD_DIM = 128
IDX_HEADS = 8
IDX_DIM = 64
RET_WIDTH = RET_HEADS * RET_HEAD_DIM
ATT_WIDTH = ATT_HEADS * ATT_HEAD_DIM
MIX_WIDTH = RET_WIDTH + ATT_WIDTH
FFN_HIDDEN = -(-(8 * D_MODEL) // (3 * 256)) * 256
TOPK_MAX = 256
ROPE_THETA = 10000.0
RET_CHUNK = 128
IDX_BLOCK = 128
ATT_BLOCK = 32
LN_EPS = 1e-5
DEEPNORM_ALPHA = (2.0 * DEPTH) ** 0.25
DEEPNORM_BETA = (8.0 * DEPTH) ** -0.25

IN_SIZES = (RET_WIDTH, RET_WIDTH, RET_WIDTH, RET_WIDTH,
            ATT_WIDTH, ATT_WIDTH, ATT_WIDTH,
            IDX_HEADS * IDX_DIM, IDX_DIM, IDX_HEADS)
IN_COLS = sum(IN_SIZES)
IN_SPLITS = tuple(int(s) for s in np.cumsum(IN_SIZES)[:-1])

kernel_name = "hybrid_retention_dsa_deepnorm"


def rope(x, positions):
    d = x.shape[-1]
    inv_freq = ROPE_THETA ** (-jnp.arange(0, d, 2, dtype=jnp.float32) / d)
    ang = positions.astype(jnp.float32)[..., None] * inv_freq
    cos = jnp.cos(ang)[:, :, None, :].astype(x.dtype)
    sin = jnp.sin(ang)[:, :, None, :].astype(x.dtype)
    x1, x2 = x[..., : d // 2], x[..., d // 2:]
    return jnp.concatenate([x1 * cos - x2 * sin, x1 * sin + x2 * cos], axis=-1)


def layer_norm(x, g, b):
    xf = x.astype(jnp.float32)
    mu = jnp.mean(xf, axis=-1, keepdims=True)
    var = jnp.mean(jnp.square(xf - mu), axis=-1, keepdims=True)
    y = (xf - mu) * lax.rsqrt(var + LN_EPS)
    return (y * g.astype(jnp.float32) + b.astype(jnp.float32)).astype(x.dtype)


def retention_chunkwise(q, k, v):
    B, S, H, d = q.shape
    C = RET_CHUNK
    N = S // C
    log_g = jnp.log(1.0 - 2.0 ** (-5.0 - jnp.arange(H, dtype=jnp.float32)))
    pos = jnp.arange(C, dtype=jnp.float32)
    rel = pos[:, None] - pos[None, :]
    decay_intra = jnp.where(rel >= 0,
                            jnp.exp(log_g[:, None, None] * jnp.maximum(rel, 0.0)),
                            0.0)
    decay_q = jnp.exp(log_g[:, None] * (pos + 1.0))
    decay_k = jnp.exp(log_g[:, None] * (C - 1.0 - pos))
    decay_chunk = jnp.exp(log_g * C)

    def to_chunks(t):
        return t.astype(jnp.float32).reshape(B, N, C, H, d).transpose(1, 0, 3, 2, 4)

    qc, kc, vc = to_chunks(q), to_chunks(k) * (d ** -0.5), to_chunks(v)
    scores = jnp.einsum('nbhid,nbhjd->nbhij', qc, kc) * decay_intra
    inner = jnp.einsum('nbhij,nbhjd->nbhid', scores, vc)
    kv = jnp.einsum('nbhjd,nbhje->nbhde', kc * decay_k[:, :, None], vc)

    def step(state, kv_n):
        return decay_chunk[:, None, None] * state + kv_n, state

    _, prev = lax.scan(step, jnp.zeros((B, H, d, d), jnp.float32), kv)
    cross = jnp.einsum('nbhid,nbhde->nbhie', qc, prev) * decay_q[:, :, None]
    out = inner + cross
    return out.transpose(1, 0, 3, 2, 4).reshape(B, S, H, d)


def indexer_topk(q_idx, k_idx, w_idx, topk):
    B, S, HI, DI = q_idx.shape
    nblk = S // IDX_BLOCK
    qf = q_idx.astype(jnp.float32) * (DI ** -0.5)
    kf = k_idx.astype(jnp.float32)
    wf = w_idx.astype(jnp.float32) * (HI ** -0.5)
    key_pos = jnp.arange(S)
    q_blocks = qf.reshape(B, nblk, IDX_BLOCK, HI, DI).swapaxes(0, 1)
    w_blocks = wf.reshape(B, nblk, IDX_BLOCK, HI).swapaxes(0, 1)
    q_pos = jnp.arange(S).reshape(nblk, IDX_BLOCK)

    def block(args):
        qb, wb, pb = args
        logits = jax.nn.relu(jnp.einsum('bqhd,bsd->bqhs', qb, kf))
        score = jnp.einsum('bqh,bqhs->bqs', wb, logits)
        score = jnp.where(key_pos[None, None, :] <= pb[None, :, None], score, -jnp.inf)
        _, sel = lax.top_k(score, topk)
        return sel.astype(jnp.int32)

    sel = lax.map(block, (q_blocks, w_blocks, q_pos))
    return sel.swapaxes(0, 1).reshape(B, S, topk)


def sparse_attention(q, k, v, sel):
    B, S, H, dh = q.shape
    K = sel.shape[-1]
    nblk = S // ATT_BLOCK
    q_blocks = q.reshape(B, nblk, ATT_BLOCK, H, dh).swapaxes(0, 1)
    s_blocks = sel.reshape(B, nblk, ATT_BLOCK, K).swapaxes(0, 1)
    q_pos = jnp.arange(S).reshape(nblk, ATT_BLOCK)
    gather = jax.vmap(lambda t, i: t[i])

    def block(args):
        qb, sb, pb = args
        kb = gather(k, sb)
        vb = gather(v, sb)
        s = jnp.einsum('bqhd,bqkhd->bhqk', qb, kb).astype(jnp.float32) * (dh ** -0.5)
        valid = sb <= pb[None, :, None]
        s = jnp.where(valid[:, None, :, :], s, -jnp.inf)
        p = jax.nn.softmax(s, axis=-1).astype(vb.dtype)
        return jnp.einsum('bhqk,bqkhd->bqhd', p, vb)

    out = lax.map(block, (q_blocks, s_blocks, q_pos))
    return out.swapaxes(0, 1).reshape(B, S, H, dh)


def hybrid_layer(x, positions, w_in, ret_gn_gain, w_out, ln_mix_gain, ln_mix_bias,
                 w_gate_up, w_down, ln_ffn_gain, ln_ffn_bias):
    B, S, _ = x.shape
    topk = min(TOPK_MAX, S // 4)
    proj = x @ w_in
    rq, rk, rv, rg, aq, ak, av, iq, ik, iw = jnp.split(proj, IN_SPLITS, axis=-1)

    rq = rope(rq.reshape(B, S, RET_HEADS, RET_HEAD_DIM), positions)
    rk = rope(rk.reshape(B, S, RET_HEADS, RET_HEAD_DIM), positions)
    rv = rv.reshape(B, S, RET_HEADS, RET_HEAD_DIM)
    ret = retention_chunkwise(rq, rk, rv)
    mu = jnp.mean(ret, axis=-1, keepdims=True)
    var = jnp.mean(jnp.square(ret - mu), axis=-1, keepdims=True)
    ret = ((ret - mu) * lax.rsqrt(var + LN_EPS)).reshape(B, S, RET_WIDTH)
    ret = (ret * ret_gn_gain.astype(jnp.float32)).astype(x.dtype)
    ret = jax.nn.silu(rg) * ret

    aq = rope(aq.reshape(B, S, ATT_HEADS, ATT_HEAD_DIM), positions)
    ak = rope(ak.reshape(B, S, ATT_HEADS, ATT_HEAD_DIM), positions)
    av = av.reshape(B, S, ATT_HEADS, ATT_HEAD_DIM)
    iq = rope(iq.reshape(B, S, IDX_HEADS, IDX_DIM), positions)
    ik = rope(ik[:, :, None, :], positions)[:, :, 0, :]
    sel = indexer_topk(iq, ik, iw, topk)
    att = sparse_attention(aq, ak, av, sel).reshape(B, S, ATT_WIDTH)

    mix = jnp.concatenate([ret, att], axis=-1) @ w_out
    x = layer_norm(DEEPNORM_ALPHA * x + mix, ln_mix_gain, ln_mix_bias)

    gate, up = jnp.split(x @ w_gate_up, 2, axis=-1)
    ffn = (jax.nn.silu(gate) * up) @ w_down
    return layer_norm(DEEPNORM_ALPHA * x + ffn, ln_ffn_gain, ln_ffn_bias)


def setup_inputs(seed: int = 0) -> dict:
    key = jax.random.key(seed)
    ks = jax.random.split(key, 12)
    f32 = jnp.float32
    x = jax.random.normal(ks[0], (BATCH, SEQ, D_MODEL), f32)
    start = jax.random.randint(ks[1], (BATCH, 1), 0, 4096, dtype=jnp.int32)
    positions = (start + jnp.arange(SEQ, dtype=jnp.int32)[None, :]).astype(jnp.int32)
    w_in = jax.random.normal(ks[2], (DEPTH, D_MODEL, IN_COLS), f32) * D_MODEL ** -0.5
    ret_gn_gain = 1.0 + 0.02 * jax.random.normal(ks[3], (DEPTH, RET_WIDTH), f32)
    w_out = jax.random.normal(ks[4], (DEPTH, MIX_WIDTH, D_MODEL), f32) * (MIX_WIDTH ** -0.5 * DEEPNORM_BETA)
    ln_mix_gain = 1.0 + 0.02 * jax.random.normal(ks[5], (DEPTH, D_MODEL), f32)
    ln_mix_bias = 0.02 * jax.random.normal(ks[6], (DEPTH, D_MODEL), f32)
    w_gate_up = jax.random.normal(ks[7], (DEPTH, D_MODEL, 2 * FFN_HIDDEN), f32) * D_MODEL ** -0.5
    w_down = jax.random.normal(ks[8], (DEPTH, FFN_HIDDEN, D_MODEL), f32) * (FFN_HIDDEN ** -0.5 * DEEPNORM_BETA)
    ln_ffn_gain = 1.0 + 0.02 * jax.random.normal(ks[9], (DEPTH, D_MODEL), f32)
    ln_ffn_bias = 0.02 * jax.random.normal(ks[10], (DEPTH, D_MODEL), f32)
    return {"x": x, "positions": positions, "w_in": w_in, "ret_gn_gain": ret_gn_gain,
            "w_out": w_out, "ln_mix_gain": ln_mix_gain, "ln_mix_bias": ln_mix_bias,
            "w_gate_up": w_gate_up, "w_down": w_down,
            "ln_ffn_gain": ln_ffn_gain, "ln_ffn_bias": ln_ffn_bias}


def reference(x, positions, w_in, ret_gn_gain, w_out, ln_mix_gain, ln_mix_bias,
              w_gate_up, w_down, ln_ffn_gain, ln_ffn_bias):
    for layer in range(DEPTH):
        x = hybrid_layer(x, positions, w_in[layer], ret_gn_gain[layer], w_out[layer],
                         ln_mix_gain[layer], ln_mix_bias[layer], w_gate_up[layer],
                         w_down[layer], ln_ffn_gain[layer], ln_ffn_bias[layer])
    return x
```

```python
import functools
import math

import jax
import jax.numpy as jnp
import numpy as np
from jax import lax
from jax.experimental import pallas as pl
from jax.experimental.pallas import tpu as pltpu

RET_HEADS = 4
ATT_HEADS = 4
HEAD_DIM = 128
IDX_HEADS = 8
IDX_DIM = 64
GROUP_WIDTH = 512
TOPK_MAX = 256
ROPE_THETA = 10000.0
RET_CHUNK = 128
LN_EPS = 1e-5

V7X_LANES = 128
V7X_VMEM_BYTES = 64 * 1024 * 1024
VMEM_LIMIT_BYTES = V7X_VMEM_BYTES - 8 * 1024 * 1024

F32 = jnp.float32
BF16 = jnp.bfloat16
INT_MIN = -(2 ** 31)
NEG_INF_KEY = INT_MIN + 0x7FFFFF


def _tiles(seq):
    t = {"proj_rows": min(512, seq), "attn_q": min(256, seq), "dense_rows": min(512, seq)}
    for v in t.values():
        assert seq % v == 0
    return t


def _params(n_axes):
    return pltpu.CompilerParams(dimension_semantics=("arbitrary",) * n_axes,
                                vmem_limit_bytes=VMEM_LIMIT_BYTES)


def _resident(shape):
    zeros = (0,) * len(shape)
    return pl.BlockSpec(shape, lambda *_: zeros, pipeline_mode=pl.Buffered(1))


def _inproj_kernel(x_ref, pos_ref, f128_ref, f64_ref, wm_ref, wavT_ref, wl_ref,
                   rq_ref, rk_ref, rv_ref, rg_ref, aq_ref, ak_ref, avT_ref,
                   iq_ref, ik_ref, iwT_ref):
    xb = x_ref[...].astype(BF16)
    pos = pos_ref[...].astype(F32)
    lane = lax.broadcasted_iota(jnp.int32, (xb.shape[0], V7X_LANES), 1)

    ang = pos * f128_ref[...]
    cos128 = jnp.cos(ang)
    sin128 = jnp.sin(ang)
    sin128 = jnp.where(lane < HEAD_DIM // 2, -sin128, sin128)
    ang = pos * f64_ref[...]
    cos64 = jnp.cos(ang)
    sin64 = jnp.sin(ang)
    first64 = (lane % IDX_DIM) < IDX_DIM // 2
    sin64 = jnp.where(first64, -sin64, sin64)

    def proj(group):
        c0 = group * GROUP_WIDTH
        return jnp.dot(xb, wm_ref[:, c0:c0 + GROUP_WIDTH], preferred_element_type=F32)

    def rope128(y, scale):
        c, s = cos128 * scale, sin128 * scale
        return y * c + pltpu.roll(y, HEAD_DIM // 2, 1) * s

    def rope64(y, scale):
        c, s = cos64 * scale, sin64 * scale
        partner = jnp.where(first64, pltpu.roll(y, V7X_LANES - IDX_DIM // 2, 1),
                            pltpu.roll(y, IDX_DIM // 2, 1))
        return y * c + partner * s

    def store_roped(out_ref, y, fn, scale):
        for h in range(GROUP_WIDTH // V7X_LANES):
            sl = slice(h * V7X_LANES, (h + 1) * V7X_LANES)
            out_ref[:, sl] = fn(y[:, sl], scale).astype(out_ref.dtype)

    qk_scale = HEAD_DIM ** -0.5
    store_roped(rq_ref, proj(0), rope128, 1.0)
    store_roped(rk_ref, proj(1), rope128, qk_scale)
    rv_ref[...] = proj(2).astype(BF16)
    rg_ref[...] = proj(3).astype(BF16)
    store_roped(aq_ref, proj(4), rope128, qk_scale)
    store_roped(ak_ref, proj(5), rope128, 1.0)
    store_roped(iq_ref, proj(7), rope64, IDX_DIM ** -0.5)

    avT = lax.dot_general(wavT_ref[...], xb, (((1,), (1,)), ((), ())),
                          preferred_element_type=F32)
    avT_ref[0] = avT.astype(BF16)

    yl = jnp.dot(xb, wl_ref[...], preferred_element_type=F32)
    ik_ref[...] = rope64(yl[:, :V7X_LANES], 1.0).astype(BF16)
    iwT = yl[:, V7X_LANES:].T
    iwT_ref[0] = iwT[:IDX_HEADS, :] * (IDX_HEADS ** -0.5)


def _inproj(x2d, pos2d, f128, f64, w_main, w_avT, w_last, batch, seq):
    m, d = x2d.shape
    tm = _tiles(seq)["proj_rows"]
    nj = seq // tm
    row = lambda b, j: (b * nj + j, 0)
    rows_spec = lambda width: pl.BlockSpec((tm, width), row)
    out_shape = [jax.ShapeDtypeStruct((m, GROUP_WIDTH), BF16)] * 6 + [
        jax.ShapeDtypeStruct((batch, GROUP_WIDTH, seq), BF16),
        jax.ShapeDtypeStruct((m, GROUP_WIDTH), BF16),
        jax.ShapeDtypeStruct((m, V7X_LANES), BF16),
        jax.ShapeDtypeStruct((batch, IDX_HEADS, seq), F32),
    ]
    out_specs = [rows_spec(GROUP_WIDTH)] * 6 + [
        pl.BlockSpec((1, GROUP_WIDTH, tm), lambda b, j: (b, 0, j)),
        rows_spec(GROUP_WIDTH),
        rows_spec(V7X_LANES),
        pl.BlockSpec((1, IDX_HEADS, tm), lambda b, j: (b, 0, j)),
    ]
    in_specs = [rows_spec(d), rows_spec(1), _resident(f128.shape), _resident(f64.shape),
                _resident(w_main.shape), _resident(w_avT.shape), _resident(w_last.shape)]
    return pl.pallas_call(
        _inproj_kernel, grid=(batch, nj), in_specs=in_specs, out_specs=out_specs,
        out_shape=out_shape, compiler_params=_params(2), name="inproj_rope",
    )(x2d, pos2d, f128, f64, w_main, w_avT, w_last)


def _retention_kernel(q_ref, k_ref, v_ref, g_ref, gain_ref, o_ref):
    seq = q_ref.shape[0]
    c = RET_CHUNK
    n_chunks = seq // c
    ri = lax.broadcasted_iota(jnp.int32, (c, c), 0).astype(F32)
    ci = lax.broadcasted_iota(jnp.int32, (c, c), 1).astype(F32)
    rel = ri - ci
    pos_col = lax.broadcasted_iota(jnp.int32, (c, 1), 0).astype(F32)

    for h in range(RET_HEADS):
        log_g = math.log(1.0 - 2.0 ** (-5.0 - h))
        decay_intra = jnp.where(rel >= 0, jnp.exp(log_g * jnp.maximum(rel, 0.0)), 0.0)
        decay_q = jnp.exp(log_g * (pos_col + 1.0))
        decay_k = jnp.exp(log_g * (c - 1.0 - pos_col))
        decay_chunk = math.exp(log_g * c)
        sl = slice(h * HEAD_DIM, (h + 1) * HEAD_DIM)
        gain = gain_ref[:, sl]

        def chunk(n, state, sl=sl, decay_intra=decay_intra, decay_q=decay_q,
                  decay_k=decay_k, decay_chunk=decay_chunk, gain=gain):
            rows = pl.ds(pl.multiple_of(n * c, c), c)
            qc = q_ref[rows, sl]
            kc = k_ref[rows, sl]
            vc = v_ref[rows, sl]
            scores = lax.dot_general(qc, kc, (((1,), (1,)), ((), ())),
                                     preferred_element_type=F32) * decay_intra
            inner = jnp.dot(scores.astype(BF16), vc, preferred_element_type=F32)
            cross = jnp.dot(qc, state.astype(BF16), preferred_element_type=F32) * decay_q
            out = inner + cross
            kdT = (kc.astype(F32) * decay_k).T.astype(BF16)
            kv = jnp.dot(kdT, vc, preferred_element_type=F32)
            new_state = decay_chunk * state + kv

            mu = jnp.mean(out, axis=-1, keepdims=True)
            cen = out - mu
            var = jnp.mean(cen * cen, axis=-1, keepdims=True)
            y = cen * lax.rsqrt(var + LN_EPS) * gain
            g = g_ref[rows, sl].astype(F32)
            o_ref[rows, sl] = (g * jax.nn.sigmoid(g) * y).astype(o_ref.dtype)
            return new_state

        lax.fori_loop(0, n_chunks, chunk, jnp.zeros((HEAD_DIM, HEAD_DIM), F32))


def _retention(rq, rk, rv, rg, gain, batch, seq):
    m = rq.shape[0]
    blk = pl.BlockSpec((seq, GROUP_WIDTH), lambda b: (b, 0))
    return pl.pallas_call(
        _retention_kernel, grid=(batch,),
        in_specs=[blk, blk, blk, blk, _resident(gain.shape)], out_specs=blk,
        out_shape=jax.ShapeDtypeStruct((m, GROUP_WIDTH), BF16),
        compiler_params=_params(1), name="retention_gn_gate",
    )(rq, rk, rv, rg, gain)


def _sortable_key(score):
    score = jnp.where(score == 0.0, 0.0, score)
    bits = lax.bitcast_convert_type(score, jnp.int32)
    return jnp.where(bits < 0, bits ^ jnp.int32(0x7FFFFFFF), bits)


def _count(mask):
    return jnp.sum(jnp.where(mask, 1.0, 0.0), axis=0, keepdims=True)


def _attn_block(qi, topk, iq_ref, ik_ref, iwT_ref, aq_ref, ak_ref, avT_ref, o_ref, key_ref):
    tq = iq_ref.shape[0]
    nk = (qi + 1) * tq
    key_row = lax.broadcasted_iota(jnp.int32, (nk, tq), 0)
    q_col = lax.broadcasted_iota(jnp.int32, (nk, tq), 1) + qi * tq
    causal = key_row <= q_col

    kk = ik_ref[0:nk, :]
    lane = lax.broadcasted_iota(jnp.int32, kk.shape, 1)
    k_lo = jnp.where(lane < IDX_DIM, kk, jnp.zeros_like(kk))
    k_hi = jnp.where(lane >= IDX_DIM, kk, jnp.zeros_like(kk))
    nt = (((1,), (1,)), ((), ()))
    score = jnp.zeros((nk, tq), F32)
    for pair in range(IDX_HEADS // 2):
        qp = iq_ref[:, pair * V7X_LANES:(pair + 1) * V7X_LANES]
        for half, kmat in enumerate((k_lo, k_hi)):
            h = 2 * pair + half
            logit = lax.dot_general(kmat, qp, nt, preferred_element_type=F32)
            score = score + jnp.maximum(logit, 0.0) * iwT_ref[0, h:h + 1, :]
    score = jnp.where(causal, score, -jnp.inf)
    key_ref[0:nk, :] = _sortable_key(score)

    kf = float(topk)
    cnt = _count(key_ref[0:nk, :] >= 0)
    prefix0 = jnp.where(cnt >= kf, jnp.int32(0), jnp.int32(INT_MIN))

    def radix_step(it, prefix):
        cand = prefix | jnp.left_shift(jnp.int32(1), 30 - it)
        cnt = _count(key_ref[0:nk, :] >= cand)
        return jnp.where(cnt >= kf, cand, prefix)

    thr = lax.fori_loop(0, 31, radix_step, prefix0)
    keys = key_ref[0:nk, :]
    gt = keys > thr
    eq = keys == thr
    need = kf - _count(gt)
    n_eq = _count(eq)

    idx_bits = max(1, (nk - 1).bit_length())

    def tie_limit():
        def step(it, prefix):
            cand = prefix + jnp.left_shift(jnp.int32(1), idx_bits - 1 - it)
            below = _count((key_ref[0:nk, :] == thr) & (key_row < cand))
            return jnp.where(below < need, cand, prefix)
        return lax.fori_loop(0, idx_bits, step, jnp.zeros((1, tq), jnp.int32))

    overfull = jnp.where((n_eq > need) & (thr > NEG_INF_KEY), 1.0, 0.0)
    limit = lax.cond(jnp.max(overfull) > 0.0, tie_limit,
                     lambda: jnp.full((1, tq), nk, jnp.int32))
    selected = (gt | (eq & (key_row <= limit))) & causal

    for h in range(ATT_HEADS):
        sl = slice(h * HEAD_DIM, (h + 1) * HEAD_DIM)
        s = lax.dot_general(ak_ref[0:nk, sl], aq_ref[:, sl], nt, preferred_element_type=F32)
        s = jnp.where(selected, s, -jnp.inf)
        mx = jnp.max(s, axis=0, keepdims=True)
        p = jnp.exp(s - mx)
        denom = jnp.sum(p, axis=0, keepdims=True)
        oT = jnp.dot(avT_ref[0, sl, 0:nk], p.astype(BF16), preferred_element_type=F32)
        o_ref[:, sl] = (oT / denom).T.astype(o_ref.dtype)


def _attn_kernel(topk, iq_ref, ik_ref, iwT_ref, aq_ref, ak_ref, avT_ref, o_ref, key_ref):
    n_q = pl.num_programs(1)
    del n_q
    qi = pl.program_id(1)
    n_blocks = ik_ref.shape[0] // iq_ref.shape[0]
    for blk in range(n_blocks):
        @pl.when(qi == blk)
        def _(blk=blk):
            _attn_block(blk, topk, iq_ref, ik_ref, iwT_ref, aq_ref, ak_ref, avT_ref, o_ref, key_ref)


def _indexed_attention(iq, ik, iwT, aq, ak, avT, batch, seq):
    m = iq.shape[0]
    tq = _tiles(seq)["attn_q"]
    nq = seq // tq
    topk = min(TOPK_MAX, seq // 4)
    q_spec = pl.BlockSpec((tq, GROUP_WIDTH), lambda b, i: (b * nq + i, 0))
    in_specs = [
        q_spec,
        pl.BlockSpec((seq, V7X_LANES), lambda b, i: (b, 0)),
        pl.BlockSpec((1, IDX_HEADS, tq), lambda b, i: (b, 0, i)),
        q_spec,
        pl.BlockSpec((seq, GROUP_WIDTH), lambda b, i: (b, 0)),
        pl.BlockSpec((1, GROUP_WIDTH, seq), lambda b, i: (b, 0, 0)),
    ]
    return pl.pallas_call(
        functools.partial(_attn_kernel, topk), grid=(batch, nq),
        in_specs=in_specs, out_specs=q_spec,
        out_shape=jax.ShapeDtypeStruct((m, GROUP_WIDTH), BF16),
        scratch_shapes=[pltpu.VMEM((seq, tq), jnp.int32)],
        compiler_params=_params(2), name="indexer_topk_attention",
    )(iq, ik, iwT, aq, ak, avT)


def _layer_norm(y, gain, bias):
    mu = jnp.mean(y, axis=-1, keepdims=True)
    cen = y - mu
    var = jnp.mean(cen * cen, axis=-1, keepdims=True)
    return cen * lax.rsqrt(var + LN_EPS) * gain + bias


def _outproj_kernel(alpha, ret_ref, att_ref, x_ref, w_ref, gain_ref, bias_ref, o_ref):
    width = ret_ref.shape[1]
    mix = jnp.dot(ret_ref[...], w_ref[0:width, :], preferred_element_type=F32)
    mix = mix + jnp.dot(att_ref[...], w_ref[width:, :], preferred_element_type=F32)
    o_ref[...] = _layer_norm(alpha * x_ref[...] + mix, gain_ref[...], bias_ref[...])


def _outproj(ret, att, x2d, w_out, gain, bias, alpha, seq):
    m, d = x2d.shape
    tm = _tiles(seq)["dense_rows"]
    rows = lambda width: pl.BlockSpec((tm, width), lambda i: (i, 0))
    return pl.pallas_call(
        functools.partial(_outproj_kernel, alpha), grid=(m // tm,),
        in_specs=[rows(GROUP_WIDTH), rows(GROUP_WIDTH), rows(d), _resident(w_out.shape),
                  _resident(gain.shape), _resident(bias.shape)],
        out_specs=rows(d), out_shape=jax.ShapeDtypeStruct((m, d), F32),
        compiler_params=_params(1), name="outproj_deepnorm",
    )(ret, att, x2d, w_out, gain, bias)


def _ffn_chunks(hidden):
    mxu_cols = 256
    assert hidden % mxu_cols == 0
    half = (hidden // mxu_cols + 1) // 2 * mxu_cols
    return [(0, half), (half, hidden)] if half < hidden else [(0, hidden)]


def _ffn_kernel(alpha, x_ref, wgu_ref, wd_ref, gain_ref, bias_ref, o_ref):
    hidden = wd_ref.shape[0]
    x = x_ref[...]
    xb = x.astype(BF16)
    acc = alpha * x
    for lo, hi in _ffn_chunks(hidden):
        gate = jnp.dot(xb, wgu_ref[:, lo:hi], preferred_element_type=F32)
        up = jnp.dot(xb, wgu_ref[:, hidden + lo:hidden + hi], preferred_element_type=F32)
        act = (gate * jax.nn.sigmoid(gate) * up).astype(BF16)
        acc = acc + jnp.dot(act, wd_ref[lo:hi, :], preferred_element_type=F32)
    o_ref[...] = _layer_norm(acc, gain_ref[...], bias_ref[...])


def _ffn(x2d, w_gate_up, w_down, gain, bias, alpha, seq):
    m, d = x2d.shape
    tm = _tiles(seq)["dense_rows"]
    rows = pl.BlockSpec((tm, d), lambda i: (i, 0))
    return pl.pallas_call(
        functools.partial(_ffn_kernel, alpha), grid=(m // tm,),
        in_specs=[rows, _resident(w_gate_up.shape), _resident(w_down.shape),
                  _resident(gain.shape), _resident(bias.shape)],
        out_specs=rows, out_shape=jax.ShapeDtypeStruct((m, d), F32),
        compiler_params=_params(1), name="ffn_deepnorm",
    )(x2d, w_gate_up, w_down, gain, bias)


def _rope_freq_tables():
    def table(dim):
        inv = ROPE_THETA ** (-jnp.arange(0, dim, 2, dtype=F32) / dim)
        return jnp.tile(inv, V7X_LANES // (dim // 2))[None, :]
    return table(HEAD_DIM), table(IDX_DIM)


def _split_w_in(w_in_l):
    g = GROUP_WIDTH
    w_main = w_in_l[:, :8 * g].astype(BF16)
    w_avT = w_in_l[:, 6 * g:7 * g].T.astype(BF16)
    ik = w_in_l[:, 8 * g:8 * g + IDX_DIM]
    iw = w_in_l[:, 8 * g + IDX_DIM:8 * g + IDX_DIM + IDX_HEADS]
    pad = jnp.zeros((w_in_l.shape[0], V7X_LANES - IDX_HEADS), w_in_l.dtype)
    w_last = jnp.concatenate([ik, ik, iw, pad], axis=1).astype(BF16)
    return w_main, w_avT, w_last


def kernel(x, positions, w_in, ret_gn_gain, w_out, ln_mix_gain, ln_mix_bias,
           w_gate_up, w_down, ln_ffn_gain, ln_ffn_bias):
    batch, seq, d = x.shape
    depth = w_in.shape[0]
    alpha = (2.0 * depth) ** 0.25
    f128, f64 = _rope_freq_tables()
    pos2d = positions.reshape(batch * seq, 1)
    x2d = x.reshape(batch * seq, d)
    for l in range(depth):
        w_main, w_avT, w_last = _split_w_in(w_in[l])
        rq, rk, rv, rg, aq, ak, avT, iq, ik, iwT = _inproj(
            x2d, pos2d, f128, f64, w_main, w_avT, w_last, batch, seq)
        ret = _retention(rq, rk, rv, rg, ret_gn_gain[l][None, :], batch, seq)
        att = _indexed_attention(iq, ik, iwT, aq, ak, avT, batch, seq)
        x2d = _outproj(ret, att, x2d, w_out[l].astype(BF16), ln_mix_gain[l][None, :],
                       ln_mix_bias[l][None, :], alpha, seq)
        x2d = _ffn(x2d, w_gate_up[l].astype(BF16), w_down[l].astype(BF16),
                   ln_ffn_gain[l][None, :], ln_ffn_bias[l][None, :], alpha, seq)
    return x2d.reshape(batch, seq, d)
```

```python
import functools
import math

import jax
import jax.numpy as jnp
from jax import lax
from jax.experimental import pallas as pl
from jax.experimental.pallas import tpu as pltpu

RET_HEADS = 4
ATT_HEADS = 4
HEAD_DIM = 128
IDX_HEADS = 8
IDX_DIM = 64
GROUP_WIDTH = 512
TOPK_MAX = 256
ROPE_THETA = 10000.0
RET_CHUNK = 128
LN_EPS = 1e-5

V7X_LANES = 128
V7X_VMEM_BYTES = 64 * 1024 * 1024
VMEM_LIMIT_BYTES = V7X_VMEM_BYTES - 8 * 1024 * 1024

F32 = jnp.float32
BF16 = jnp.bfloat16
INT_MIN = -(2 ** 31)
NEG_INF_KEY = INT_MIN + 0x7FFFFF
MASKED = -1e30


def _tiles(seq):
    t = {"proj_rows": min(512, seq), "attn_q": min(256, seq), "attn_keys": min(256, seq),
         "dense_rows": min(512, seq)}
    for v in t.values():
        assert seq % v == 0
    assert t["proj_rows"] % t["attn_keys"] == 0 and t["attn_q"] % t["attn_keys"] == 0
    return t


def _params(n_axes):
    return pltpu.CompilerParams(dimension_semantics=("arbitrary",) * n_axes,
                                vmem_limit_bytes=VMEM_LIMIT_BYTES)


def _resident(shape):
    zeros = (0,) * len(shape)
    return pl.BlockSpec(shape, lambda *_: zeros, pipeline_mode=pl.Buffered(1))


def _inproj_kernel(x_ref, pos_ref, f128_ref, f64_ref, wm_ref, wavT_ref, wl_ref,
                   rq_ref, rk_ref, rv_ref, rg_ref, aq_ref, ak_ref, avT_ref,
                   iq_ref, ik_ref, iwT_ref):
    xb = x_ref[...].astype(BF16)
    pos = pos_ref[...].astype(F32)
    lane = lax.broadcasted_iota(jnp.int32, (xb.shape[0], V7X_LANES), 1)

    ang = pos * f128_ref[...]
    cos128 = jnp.cos(ang)
    sin128 = jnp.sin(ang)
    sin128 = jnp.where(lane < HEAD_DIM // 2, -sin128, sin128)
    ang = pos * f64_ref[...]
    cos64 = jnp.cos(ang)
    sin64 = jnp.sin(ang)
    first64 = (lane % IDX_DIM) < IDX_DIM // 2
    sin64 = jnp.where(first64, -sin64, sin64)

    def proj(group):
        c0 = group * GROUP_WIDTH
        return jnp.dot(xb, wm_ref[:, c0:c0 + GROUP_WIDTH], preferred_element_type=F32)

    def rope128(y, scale):
        c, s = cos128 * scale, sin128 * scale
        return y * c + pltpu.roll(y, HEAD_DIM // 2, 1) * s

    def rope64(y, scale):
        c, s = cos64 * scale, sin64 * scale
        partner = jnp.where(first64, pltpu.roll(y, V7X_LANES - IDX_DIM // 2, 1),
                            pltpu.roll(y, IDX_DIM // 2, 1))
        return y * c + partner * s

    def store_roped(out_ref, y, fn, scale):
        for h in range(GROUP_WIDTH // V7X_LANES):
            sl = slice(h * V7X_LANES, (h + 1) * V7X_LANES)
            out_ref[:, sl] = fn(y[:, sl], scale).astype(out_ref.dtype)

    qk_scale = HEAD_DIM ** -0.5
    store_roped(rq_ref, proj(0), rope128, 1.0)
    store_roped(rk_ref, proj(1), rope128, qk_scale)
    rv_ref[...] = proj(2).astype(BF16)
    rg_ref[...] = proj(3).astype(BF16)
    store_roped(aq_ref, proj(4), rope128, qk_scale)
    store_roped(ak_ref, proj(5), rope128, 1.0)
    store_roped(iq_ref, proj(7), rope64, IDX_DIM ** -0.5)

    avT = lax.dot_general(wavT_ref[...], xb, (((1,), (1,)), ((), ())),
                          preferred_element_type=F32)
    kb = avT_ref.shape[3]
    for j in range(avT_ref.shape[1]):
        avT_ref[0, j] = avT[:, j * kb:(j + 1) * kb].astype(BF16)

    yl = jnp.dot(xb, wl_ref[...], preferred_element_type=F32)
    ik_ref[...] = rope64(yl[:, :V7X_LANES], 1.0).astype(BF16)
    iwT = yl[:, V7X_LANES:].T
    iwT_ref[0] = iwT[:IDX_HEADS, :] * (IDX_HEADS ** -0.5)


def _inproj(x2d, pos2d, f128, f64, w_main, w_avT, w_last, batch, seq):
    m, d = x2d.shape
    tm = _tiles(seq)["proj_rows"]
    kb = _tiles(seq)["attn_keys"]
    nj = seq // tm
    row = lambda b, j: (b * nj + j, 0)
    rows_spec = lambda width: pl.BlockSpec((tm, width), row)
    out_shape = [jax.ShapeDtypeStruct((m, GROUP_WIDTH), BF16)] * 6 + [
        jax.ShapeDtypeStruct((batch, seq // kb, GROUP_WIDTH, kb), BF16),
        jax.ShapeDtypeStruct((m, GROUP_WIDTH), BF16),
        jax.ShapeDtypeStruct((m, V7X_LANES), BF16),
        jax.ShapeDtypeStruct((batch, IDX_HEADS, seq), F32),
    ]
    out_specs = [rows_spec(GROUP_WIDTH)] * 6 + [
        pl.BlockSpec((1, tm // kb, GROUP_WIDTH, kb), lambda b, j: (b, j, 0, 0)),
        rows_spec(GROUP_WIDTH),
        rows_spec(V7X_LANES),
        pl.BlockSpec((1, IDX_HEADS, tm), lambda b, j: (b, 0, j)),
    ]
    in_specs = [rows_spec(d), rows_spec(1), _resident(f128.shape), _resident(f64.shape),
                _resident(w_main.shape), _resident(w_avT.shape), _resident(w_last.shape)]
    return pl.pallas_call(
        _inproj_kernel, grid=(batch, nj), in_specs=in_specs, out_specs=out_specs,
        out_shape=out_shape, compiler_params=_params(2), name="inproj_rope",
    )(x2d, pos2d, f128, f64, w_main, w_avT, w_last)


def _retention_kernel(q_ref, k_ref, v_ref, g_ref, gain_ref, o_ref):
    seq = q_ref.shape[0]
    c = RET_CHUNK
    n_chunks = seq // c
    ri = lax.broadcasted_iota(jnp.int32, (c, c), 0).astype(F32)
    ci = lax.broadcasted_iota(jnp.int32, (c, c), 1).astype(F32)
    rel = ri - ci
    pos_col = lax.broadcasted_iota(jnp.int32, (c, 1), 0).astype(F32)

    for h in range(RET_HEADS):
        log_g = math.log(1.0 - 2.0 ** (-5.0 - h))
        decay_intra = jnp.where(rel >= 0, jnp.exp(log_g * jnp.maximum(rel, 0.0)), 0.0)
        decay_q = jnp.exp(log_g * (pos_col + 1.0))
        decay_k = jnp.exp(log_g * (c - 1.0 - pos_col))
        decay_chunk = math.exp(log_g * c)
        sl = slice(h * HEAD_DIM, (h + 1) * HEAD_DIM)
        gain = gain_ref[:, sl]

        def chunk(n, state, sl=sl, decay_intra=decay_intra, decay_q=decay_q,
                  decay_k=decay_k, decay_chunk=decay_chunk, gain=gain):
            rows = pl.ds(pl.multiple_of(n * c, c), c)
            qc = q_ref[rows, sl]
            kc = k_ref[rows, sl]
            vc = v_ref[rows, sl]
            scores = lax.dot_general(qc, kc, (((1,), (1,)), ((), ())),
                                     preferred_element_type=F32) * decay_intra
            inner = jnp.dot(scores.astype(BF16), vc, preferred_element_type=F32)
            cross = jnp.dot(qc, state.astype(BF16), preferred_element_type=F32) * decay_q
            out = inner + cross
            kdT = (kc.astype(F32) * decay_k).T.astype(BF16)
            kv = jnp.dot(kdT, vc, preferred_element_type=F32)
            new_state = decay_chunk * state + kv

            mu = jnp.mean(out, axis=-1, keepdims=True)
            cen = out - mu
            var = jnp.mean(cen * cen, axis=-1, keepdims=True)
            y = cen * lax.rsqrt(var + LN_EPS) * gain
            g = g_ref[rows, sl].astype(F32)
            o_ref[rows, sl] = (g * jax.nn.sigmoid(g) * y).astype(o_ref.dtype)
            return new_state

        lax.fori_loop(0, n_chunks, chunk, jnp.zeros((HEAD_DIM, HEAD_DIM), F32))


def _retention(rq, rk, rv, rg, gain, batch, seq):
    m = rq.shape[0]
    blk = pl.BlockSpec((seq, GROUP_WIDTH), lambda b: (b, 0))
    return pl.pallas_call(
        _retention_kernel, grid=(batch,),
        in_specs=[blk, blk, blk, blk, _resident(gain.shape)], out_specs=blk,
        out_shape=jax.ShapeDtypeStruct((m, GROUP_WIDTH), BF16),
        compiler_params=_params(1), name="retention_gn_gate",
    )(rq, rk, rv, rg, gain)


def _sortable_key(score):
    score = jnp.where(score == 0.0, 0.0, score)
    bits = lax.bitcast_convert_type(score, jnp.int32)
    return jnp.where(bits < 0, bits ^ jnp.int32(0x7FFFFFFF), bits)


def _attn_kernel(topk, iq_ref, ik_ref, iwT_ref, aq_ref, ak_ref, avT_ref, o_ref,
                 key_ref, m_ref, l_ref, acc_ref):
    tq = iq_ref.shape[0]
    kb = avT_ref.shape[3]
    qi = pl.program_id(1)
    n_blk = (qi * tq) // kb + 1
    kf = float(topk)
    nt = (((1,), (1,)), ((), ()))
    row = lax.broadcasted_iota(jnp.int32, (kb, tq), 0)
    q_pos = lax.broadcasted_iota(jnp.int32, (kb, tq), 1) + qi * tq

    def rows_of(j):
        return pl.ds(pl.multiple_of(j * kb, kb), kb)

    def index_block(j, carry):
        kk = ik_ref[rows_of(j), :]
        lane = lax.broadcasted_iota(jnp.int32, kk.shape, 1)
        k_lo = jnp.where(lane < IDX_DIM, kk, jnp.zeros_like(kk))
        k_hi = jnp.where(lane >= IDX_DIM, kk, jnp.zeros_like(kk))
        score = jnp.zeros((kb, tq), F32)
        for pair in range(IDX_HEADS // 2):
            qp = iq_ref[:, pair * V7X_LANES:(pair + 1) * V7X_LANES]
            for half, kmat in enumerate((k_lo, k_hi)):
                h = 2 * pair + half
                logit = lax.dot_general(kmat, qp, nt, preferred_element_type=F32)
                score = score + jnp.maximum(logit, 0.0) * iwT_ref[0, h:h + 1, :]
        score = jnp.where(row + j * kb <= q_pos, score, -jnp.inf)
        key_ref[rows_of(j), :] = _sortable_key(score)
        return carry

    lax.fori_loop(0, n_blk, index_block, 0)

    def count(pred):
        def body(j, acc):
            ones = jnp.where(pred(key_ref[rows_of(j), :], row + j * kb), 1.0, 0.0)
            return acc + jnp.sum(ones.reshape(kb // 8, 8, tq), axis=0)
        acc = lax.fori_loop(0, n_blk, body, jnp.zeros((8, tq), F32))
        return jnp.sum(acc, axis=0, keepdims=True)

    prefix0 = jnp.where(count(lambda k, r: k >= 0) >= kf, jnp.int32(0), jnp.int32(INT_MIN))

    def radix_step(it, prefix):
        cand = prefix | jnp.left_shift(jnp.int32(1), 30 - it)
        return jnp.where(count(lambda k, r: k >= cand) >= kf, cand, prefix)

    thr = lax.fori_loop(0, 31, radix_step, prefix0)
    need = kf - count(lambda k, r: k > thr)
    n_eq = count(lambda k, r: k == thr)

    idx_bits = max(1, (key_ref.shape[0] - 1).bit_length())

    def tie_limit():
        def step(it, prefix):
            cand = prefix + jnp.left_shift(jnp.int32(1), idx_bits - 1 - it)
            below = count(lambda k, r: (k == thr) & (r < cand))
            return jnp.where(below < need, cand, prefix)
        return lax.fori_loop(0, idx_bits, step, jnp.zeros((1, tq), jnp.int32))

    overfull = jnp.where((n_eq > need) & (thr > NEG_INF_KEY), 1.0, 0.0)
    limit = lax.cond(jnp.max(overfull) > 0.0, tie_limit,
                     lambda: jnp.full((1, tq), key_ref.shape[0], jnp.int32))

    m_ref[...] = jnp.full(m_ref.shape, MASKED, F32)
    l_ref[...] = jnp.zeros(l_ref.shape, F32)
    acc_ref[...] = jnp.zeros(acc_ref.shape, F32)

    def attend_block(j, carry):
        keys = key_ref[rows_of(j), :]
        key_pos = row + j * kb
        selected = ((keys > thr) | ((keys == thr) & (key_pos <= limit))) & (key_pos <= q_pos)
        bias = jnp.where(selected, 0.0, MASKED)
        for h in range(ATT_HEADS):
            sl = slice(h * HEAD_DIM, (h + 1) * HEAD_DIM)
            s = lax.dot_general(ak_ref[rows_of(j), sl], aq_ref[:, sl], nt,
                                preferred_element_type=F32) + bias
            m_old = m_ref[h:h + 1, :]
            m_new = jnp.maximum(m_old, jnp.max(s, axis=0, keepdims=True))
            p = jnp.exp(s - m_new)
            rescale = jnp.exp(m_old - m_new)
            l_ref[h:h + 1, :] = rescale * l_ref[h:h + 1, :] + jnp.sum(p, axis=0, keepdims=True)
            pv = jnp.dot(avT_ref[0, j, sl, :], p.astype(BF16), preferred_element_type=F32)
            acc_ref[h] = rescale * acc_ref[h] + pv
            m_ref[h:h + 1, :] = m_new
        return carry

    lax.fori_loop(0, n_blk, attend_block, 0)
    for h in range(ATT_HEADS):
        sl = slice(h * HEAD_DIM, (h + 1) * HEAD_DIM)
        o_ref[:, sl] = (acc_ref[h] / l_ref[h:h + 1, :]).T.astype(o_ref.dtype)


def _indexed_attention(iq, ik, iwT, aq, ak, avT, batch, seq):
    m = iq.shape[0]
    tq = _tiles(seq)["attn_q"]
    kb = avT.shape[3]
    nq = seq // tq
    topk = min(TOPK_MAX, seq // 4)
    q_spec = pl.BlockSpec((tq, GROUP_WIDTH), lambda b, i: (b * nq + i, 0))
    in_specs = [
        q_spec,
        pl.BlockSpec((seq, V7X_LANES), lambda b, i: (b, 0)),
        pl.BlockSpec((1, IDX_HEADS, tq), lambda b, i: (b, 0, i)),
        q_spec,
        pl.BlockSpec((seq, GROUP_WIDTH), lambda b, i: (b, 0)),
        pl.BlockSpec((1, seq // kb, GROUP_WIDTH, kb), lambda b, i: (b, 0, 0, 0)),
    ]
    scratch = [
        pltpu.VMEM((seq, tq), jnp.int32),
        pltpu.VMEM((ATT_HEADS, tq), F32),
        pltpu.VMEM((ATT_HEADS, tq), F32),
        pltpu.VMEM((ATT_HEADS, HEAD_DIM, tq), F32),
    ]
    return pl.pallas_call(
        functools.partial(_attn_kernel, topk), grid=(batch, nq),
        in_specs=in_specs, out_specs=q_spec,
        out_shape=jax.ShapeDtypeStruct((m, GROUP_WIDTH), BF16),
        scratch_shapes=scratch,
        compiler_params=_params(2), name="indexer_topk_attention",
    )(iq, ik, iwT, aq, ak, avT)


def _layer_norm(y, gain, bias):
    mu = jnp.mean(y, axis=-1, keepdims=True)
    cen = y - mu
    var = jnp.mean(cen * cen, axis=-1, keepdims=True)
    return cen * lax.rsqrt(var + LN_EPS) * gain + bias


def _outproj_kernel(alpha, ret_ref, att_ref, x_ref, w_ref, gain_ref, bias_ref, o_ref):
    width = ret_ref.shape[1]
    mix = jnp.dot(ret_ref[...], w_ref[0:width, :], preferred_element_type=F32)
    mix = mix + jnp.dot(att_ref[...], w_ref[width:, :], preferred_element_type=F32)
    o_ref[...] = _layer_norm(alpha * x_ref[...] + mix, gain_ref[...], bias_ref[...])


def _outproj(ret, att, x2d, w_out, gain, bias, alpha, seq):
    m, d = x2d.shape
    tm = _tiles(seq)["dense_rows"]
    rows = lambda width: pl.BlockSpec((tm, width), lambda i: (i, 0))
    return pl.pallas_call(
        functools.partial(_outproj_kernel, alpha), grid=(m // tm,),
        in_specs=[rows(GROUP_WIDTH), rows(GROUP_WIDTH), rows(d), _resident(w_out.shape),
                  _resident(gain.shape), _resident(bias.shape)],
        out_specs=rows(d), out_shape=jax.ShapeDtypeStruct((m, d), F32),
        compiler_params=_params(1), name="outproj_deepnorm",
    )(ret, att, x2d, w_out, gain, bias)


def _ffn_chunks(hidden):
    mxu_cols = 256
    assert hidden % mxu_cols == 0
    half = (hidden // mxu_cols + 1) // 2 * mxu_cols
    return [(0, half), (half, hidden)] if half < hidden else [(0, hidden)]


def _ffn_kernel(alpha, x_ref, wgu_ref, wd_ref, gain_ref, bias_ref, o_ref):
    hidden = wd_ref.shape[0]
    x = x_ref[...]
    xb = x.astype(BF16)
    acc = alpha * x
    for lo, hi in _ffn_chunks(hidden):
        gate = jnp.dot(xb, wgu_ref[:, lo:hi], preferred_element_type=F32)
        up = jnp.dot(xb, wgu_ref[:, hidden + lo:hidden + hi], preferred_element_type=F32)
        act = (gate * jax.nn.sigmoid(gate) * up).astype(BF16)
        acc = acc + jnp.dot(act, wd_ref[lo:hi, :], preferred_element_type=F32)
    o_ref[...] = _layer_norm(acc, gain_ref[...], bias_ref[...])


def _ffn(x2d, w_gate_up, w_down, gain, bias, alpha, seq):
    m, d = x2d.shape
    tm = _tiles(seq)["dense_rows"]
    rows = pl.BlockSpec((tm, d), lambda i: (i, 0))
    return pl.pallas_call(
        functools.partial(_ffn_kernel, alpha), grid=(m // tm,),
        in_specs=[rows, _resident(w_gate_up.shape), _resident(w_down.shape),
                  _resident(gain.shape), _resident(bias.shape)],
        out_specs=rows, out_shape=jax.ShapeDtypeStruct((m, d), F32),
        compiler_params=_params(1), name="ffn_deepnorm",
    )(x2d, w_gate_up, w_down, gain, bias)


def _rope_freq_tables():
    def table(dim):
        inv = ROPE_THETA ** (-jnp.arange(0, dim, 2, dtype=F32) / dim)
        return jnp.tile(inv, V7X_LANES // (dim // 2))[None, :]
    return table(HEAD_DIM), table(IDX_DIM)


def _split_w_in(w_in_l):
    g = GROUP_WIDTH
    w_main = w_in_l[:, :8 * g].astype(BF16)
    w_avT = w_in_l[:, 6 * g:7 * g].T.astype(BF16)
    ik = w_in_l[:, 8 * g:8 * g + IDX_DIM]
    iw = w_in_l[:, 8 * g + IDX_DIM:8 * g + IDX_DIM + IDX_HEADS]
    pad = jnp.zeros((w_in_l.shape[0], V7X_LANES - IDX_HEADS), w_in_l.dtype)
    w_last = jnp.concatenate([ik, ik, iw, pad], axis=1).astype(BF16)
    return w_main, w_avT, w_last


def kernel(x, positions, w_in, ret_gn_gain, w_out, ln_mix_gain, ln_mix_bias,
           w_gate_up, w_down, ln_ffn_gain, ln_ffn_bias):
    batch, seq, d = x.shape
    depth = w_in.shape[0]
    alpha = (2.0 * depth) ** 0.25
    f128, f64 = _rope_freq_tables()
    pos2d = positions.reshape(batch * seq, 1)
    x2d = x.reshape(batch * seq, d)
    for l in range(depth):
        w_main, w_avT, w_last = _split_w_in(w_in[l])
        rq, rk, rv, rg, aq, ak, avT, iq, ik, iwT = _inproj(
            x2d, pos2d, f128, f64, w_main, w_avT, w_last, batch, seq)
        ret = _retention(rq, rk, rv, rg, ret_gn_gain[l][None, :], batch, seq)
        att = _indexed_attention(iq, ik, iwT, aq, ak, avT, batch, seq)
        x2d = _outproj(ret, att, x2d, w_out[l].astype(BF16), ln_mix_gain[l][None, :],
                       ln_mix_bias[l][None, :], alpha, seq)
        x2d = _ffn(x2d, w_gate_up[l].astype(BF16), w_down[l].astype(BF16),
                   ln_ffn_gain[l][None, :], ln_ffn_bias[l][None, :], alpha, seq)
    return x2d.reshape(batch, seq, d)
```

```python
import functools
import math

import jax
import jax.numpy as jnp
from jax import lax
from jax.experimental import pallas as pl
from jax.experimental.pallas import tpu as pltpu

RET_HEADS = 4
ATT_HEADS = 4
HEAD_DIM = 128
IDX_HEADS = 8
IDX_DIM = 64
GROUP_WIDTH = 512
TOPK_MAX = 256
ROPE_THETA = 10000.0
RET_CHUNK = 128
LN_EPS = 1e-5

V7X_LANES = 128
BF16_TILE_ROWS = 16
V7X_VMEM_BYTES = 64 * 1024 * 1024
VMEM_LIMIT_BYTES = V7X_VMEM_BYTES - 8 * 1024 * 1024

F32 = jnp.float32
BF16 = jnp.bfloat16
INT_MIN = -(2 ** 31)
MASKED = -1e30


def _tiles(seq):
    t = {"proj_rows": min(512, seq), "attn_q": min(256, seq), "attn_keys": min(256, seq),
         "dense_rows": min(512, seq)}
    for v in t.values():
        assert seq % v == 0
    assert t["proj_rows"] % t["attn_keys"] == 0 and t["attn_q"] % t["attn_keys"] == 0
    return t


def _params(n_axes):
    return pltpu.CompilerParams(dimension_semantics=("arbitrary",) * n_axes,
                                vmem_limit_bytes=VMEM_LIMIT_BYTES)


def _resident(shape):
    zeros = (0,) * len(shape)
    return pl.BlockSpec(shape, lambda *_: zeros, pipeline_mode=pl.Buffered(1))


def _inproj_kernel(x_ref, pos_ref, f128_ref, f64_ref, wm_ref, wavT_ref, wl_ref,
                   rq_ref, rk_ref, rv_ref, rg_ref, aq_ref, ak_ref, avT_ref,
                   iq_ref, ik_ref, iwT_ref):
    xb = x_ref[...].astype(BF16)
    pos = pos_ref[...].astype(F32)
    lane = lax.broadcasted_iota(jnp.int32, (xb.shape[0], V7X_LANES), 1)

    ang = pos * f128_ref[...]
    cos128 = jnp.cos(ang)
    sin128 = jnp.sin(ang)
    sin128 = jnp.where(lane < HEAD_DIM // 2, -sin128, sin128)
    ang = pos * f64_ref[...]
    cos64 = jnp.cos(ang)
    sin64 = jnp.sin(ang)
    first64 = (lane % IDX_DIM) < IDX_DIM // 2
    sin64 = jnp.where(first64, -sin64, sin64)

    def proj(group):
        c0 = group * GROUP_WIDTH
        return jnp.dot(xb, wm_ref[:, c0:c0 + GROUP_WIDTH], preferred_element_type=F32)

    def rope128(y, scale):
        c, s = cos128 * scale, sin128 * scale
        return y * c + pltpu.roll(y, HEAD_DIM // 2, 1) * s

    def rope64(y, scale):
        c, s = cos64 * scale, sin64 * scale
        partner = jnp.where(first64, pltpu.roll(y, V7X_LANES - IDX_DIM // 2, 1),
                            pltpu.roll(y, IDX_DIM // 2, 1))
        return y * c + partner * s

    def store_roped(out_ref, y, fn, scale):
        for h in range(GROUP_WIDTH // V7X_LANES):
            sl = slice(h * V7X_LANES, (h + 1) * V7X_LANES)
            out_ref[:, sl] = fn(y[:, sl], scale).astype(out_ref.dtype)

    qk_scale = HEAD_DIM ** -0.5
    store_roped(rq_ref, proj(0), rope128, 1.0)
    store_roped(rk_ref, proj(1), rope128, qk_scale)
    rv_ref[...] = proj(2).astype(BF16)
    rg_ref[...] = proj(3).astype(BF16)
    store_roped(aq_ref, proj(4), rope128, qk_scale)
    store_roped(ak_ref, proj(5), rope128, 1.0)
    store_roped(iq_ref, proj(7), rope64, IDX_DIM ** -0.5)

    avT = lax.dot_general(wavT_ref[...], xb, (((1,), (1,)), ((), ())),
                          preferred_element_type=F32)
    kb = avT_ref.shape[3]
    for j in range(avT_ref.shape[1]):
        avT_ref[0, j] = avT[:, j * kb:(j + 1) * kb].astype(BF16)

    yl = jnp.dot(xb, wl_ref[...], preferred_element_type=F32)
    ik_ref[...] = rope64(yl[:, :V7X_LANES], 1.0).astype(BF16)
    iwT = yl[:, V7X_LANES:].T
    iwT_ref[0] = iwT[:IDX_HEADS, :] * (IDX_HEADS ** -0.5)


def _inproj(x2d, pos2d, f128, f64, w_main, w_avT, w_last, batch, seq):
    m, d = x2d.shape
    tm = _tiles(seq)["proj_rows"]
    kb = _tiles(seq)["attn_keys"]
    nj = seq // tm
    row = lambda b, j: (b * nj + j, 0)
    rows_spec = lambda width: pl.BlockSpec((tm, width), row)
    out_shape = [jax.ShapeDtypeStruct((m, GROUP_WIDTH), BF16)] * 6 + [
        jax.ShapeDtypeStruct((batch, seq // kb, GROUP_WIDTH, kb), BF16),
        jax.ShapeDtypeStruct((m, GROUP_WIDTH), BF16),
        jax.ShapeDtypeStruct((m, V7X_LANES), BF16),
        jax.ShapeDtypeStruct((batch, IDX_HEADS, seq), F32),
    ]
    out_specs = [rows_spec(GROUP_WIDTH)] * 6 + [
        pl.BlockSpec((1, tm // kb, GROUP_WIDTH, kb), lambda b, j: (b, j, 0, 0)),
        rows_spec(GROUP_WIDTH),
        rows_spec(V7X_LANES),
        pl.BlockSpec((1, IDX_HEADS, tm), lambda b, j: (b, 0, j)),
    ]
    in_specs = [rows_spec(d), rows_spec(1), _resident(f128.shape), _resident(f64.shape),
                _resident(w_main.shape), _resident(w_avT.shape), _resident(w_last.shape)]
    return pl.pallas_call(
        _inproj_kernel, grid=(batch, nj), in_specs=in_specs, out_specs=out_specs,
        out_shape=out_shape, compiler_params=_params(2), name="inproj_rope",
    )(x2d, pos2d, f128, f64, w_main, w_avT, w_last)


def _retention_kernel(q_ref, k_ref, v_ref, g_ref, gain_ref, o_ref):
    seq = q_ref.shape[0]
    c = RET_CHUNK
    n_chunks = seq // c
    ri = lax.broadcasted_iota(jnp.int32, (c, c), 0).astype(F32)
    ci = lax.broadcasted_iota(jnp.int32, (c, c), 1).astype(F32)
    rel = ri - ci
    pos_col = lax.broadcasted_iota(jnp.int32, (c, 1), 0).astype(F32)

    log_g = [math.log(1.0 - 2.0 ** (-5.0 - h)) for h in range(RET_HEADS)]
    decay_intra = [jnp.where(rel >= 0, jnp.exp(lg * jnp.maximum(rel, 0.0)), 0.0) for lg in log_g]
    decay_q = [jnp.exp(lg * (pos_col + 1.0)) for lg in log_g]
    decay_k = [jnp.exp(lg * (c - 1.0 - pos_col)) for lg in log_g]

    def chunk(n, states):
        rows = pl.ds(pl.multiple_of(n * c, c), c)
        new_states = []
        for h in range(RET_HEADS):
            sl = slice(h * HEAD_DIM, (h + 1) * HEAD_DIM)
            qc = q_ref[rows, sl]
            kc = k_ref[rows, sl]
            vc = v_ref[rows, sl]
            scores = lax.dot_general(qc, kc, (((1,), (1,)), ((), ())),
                                     preferred_element_type=F32) * decay_intra[h]
            inner = jnp.dot(scores.astype(BF16), vc, preferred_element_type=F32)
            cross = jnp.dot(qc, states[h].astype(BF16), preferred_element_type=F32) * decay_q[h]
            out = inner + cross
            kdT = (kc.astype(F32) * decay_k[h]).T.astype(BF16)
            kv = jnp.dot(kdT, vc, preferred_element_type=F32)
            new_states.append(math.exp(log_g[h] * c) * states[h] + kv)

            mu = jnp.mean(out, axis=-1, keepdims=True)
            cen = out - mu
            var = jnp.mean(cen * cen, axis=-1, keepdims=True)
            y = cen * lax.rsqrt(var + LN_EPS) * gain_ref[:, sl]
            g = g_ref[rows, sl].astype(F32)
            o_ref[rows, sl] = (g * jax.nn.sigmoid(g) * y).astype(o_ref.dtype)
        return tuple(new_states)

    zero = jnp.zeros((HEAD_DIM, HEAD_DIM), F32)
    lax.fori_loop(0, n_chunks, chunk, (zero,) * RET_HEADS)


def _retention(rq, rk, rv, rg, gain, batch, seq):
    m = rq.shape[0]
    blk = pl.BlockSpec((seq, GROUP_WIDTH), lambda b: (b, 0))
    return pl.pallas_call(
        _retention_kernel, grid=(batch,),
        in_specs=[blk, blk, blk, blk, _resident(gain.shape)], out_specs=blk,
        out_shape=jax.ShapeDtypeStruct((m, GROUP_WIDTH), BF16),
        compiler_params=_params(1), name="retention_gn_gate",
    )(rq, rk, rv, rg, gain)


def _sortable_key(score):
    score = jnp.where(score == 0.0, 0.0, score)
    bits = lax.bitcast_convert_type(score, jnp.int32)
    return jnp.where(bits < 0, bits ^ jnp.int32(0x7FFFFFFF), bits)


def _attn_kernel(topk, iq_ref, ik_ref, iwT_ref, aq_ref, ak_ref, avT_ref, o_ref,
                 key_ref, acc_ref):
    tq = iq_ref.shape[0]
    kb = avT_ref.shape[3]
    qi = pl.program_id(1)
    n_blk = (qi * tq) // kb + 1
    kf = float(topk)
    nt = (((1,), (1,)), ((), ()))
    row = lax.broadcasted_iota(jnp.int32, (kb, tq), 0)
    q_pos = lax.broadcasted_iota(jnp.int32, (kb, tq), 1) + qi * tq

    def rows_of(j):
        return pl.ds(pl.multiple_of(j * kb, kb), kb)

    def index_block(j, carry):
        kk = ik_ref[rows_of(j), :]
        lane = lax.broadcasted_iota(jnp.int32, kk.shape, 1)
        k_lo = jnp.where(lane < IDX_DIM, kk, jnp.zeros_like(kk))
        k_hi = jnp.where(lane >= IDX_DIM, kk, jnp.zeros_like(kk))
        logits = []
        for pair in range(IDX_HEADS // 2):
            qp = iq_ref[:, pair * V7X_LANES:(pair + 1) * V7X_LANES]
            logits += [lax.dot_general(kmat, qp, nt, preferred_element_type=F32)
                       for kmat in (k_lo, k_hi)]
        score = jnp.zeros((kb, tq), F32)
        for h in range(IDX_HEADS):
            score = score + jnp.maximum(logits[h], 0.0) * iwT_ref[0, h:h + 1, :]
        key_ref[rows_of(j), :] = jnp.where(row + j * kb <= q_pos, _sortable_key(score),
                                           jnp.int32(INT_MIN))
        return carry

    lax.fori_loop(0, n_blk, index_block, 0)

    def count(pred):
        def body(j, acc):
            ones = jnp.where(pred(key_ref[rows_of(j), :], row + j * kb), 1.0, 0.0)
            return acc + jnp.sum(ones.reshape(kb // 8, 8, tq), axis=0)
        acc = lax.fori_loop(0, n_blk, body, jnp.zeros((8, tq), F32))
        return jnp.sum(acc, axis=0, keepdims=True)

    prefix0 = jnp.where(count(lambda k, r: k >= 0) >= kf, jnp.int32(0), jnp.int32(INT_MIN))

    def radix_step(it, prefix):
        cand = prefix | jnp.left_shift(jnp.int32(1), 30 - it)
        return jnp.where(count(lambda k, r: k >= cand) >= kf, cand, prefix)

    thr = lax.fori_loop(0, 31, radix_step, prefix0)
    need = kf - count(lambda k, r: k > thr)
    n_eq = count(lambda k, r: k == thr)

    idx_bits = max(1, (key_ref.shape[0] - 1).bit_length())

    def tie_limit():
        def step(it, prefix):
            cand = prefix + jnp.left_shift(jnp.int32(1), idx_bits - 1 - it)
            below = count(lambda k, r: (k == thr) & (r < cand))
            return jnp.where(below < need, cand, prefix)
        return lax.fori_loop(0, idx_bits, step, jnp.zeros((1, tq), jnp.int32))

    overfull = jnp.where((n_eq > need) & (thr > INT_MIN), 1.0, 0.0)
    has_ties = jnp.max(overfull) > 0.0

    def attend(is_selected):
        acc_ref[...] = jnp.zeros(acc_ref.shape, F32)
        ones_rows = jnp.ones((BF16_TILE_ROWS, kb), BF16)

        def block(j, carry):
            ms, ls = carry
            bias = jnp.where(is_selected(key_ref[rows_of(j), :], row + j * kb), 0.0, MASKED)
            heads = [slice(h * HEAD_DIM, (h + 1) * HEAD_DIM) for h in range(ATT_HEADS)]
            logits = [lax.dot_general(ak_ref[rows_of(j), sl], aq_ref[:, sl], nt,
                                      preferred_element_type=F32) for sl in heads]
            new_ms, probs, rescales = [], [], []
            for h in range(ATT_HEADS):
                s = logits[h] + bias
                m_new = jnp.maximum(ms[h], jnp.max(s, axis=0, keepdims=True))
                probs.append(jnp.exp(s - m_new).astype(BF16))
                rescales.append(jnp.exp(ms[h] - m_new))
                new_ms.append(m_new)
            new_ls = []
            for h, sl in enumerate(heads):
                v_aug = jnp.concatenate([avT_ref[0, j, sl, :], ones_rows], axis=0)
                pv = jnp.dot(v_aug, probs[h], preferred_element_type=F32)
                acc_ref[h] = rescales[h] * acc_ref[h] + pv[:HEAD_DIM]
                new_ls.append(rescales[h] * ls[h] + pv[HEAD_DIM:HEAD_DIM + 1])
            return tuple(new_ms), tuple(new_ls)

        init = ((jnp.full((1, tq), MASKED, F32),) * ATT_HEADS,
                (jnp.zeros((1, tq), F32),) * ATT_HEADS)
        _, ls = lax.fori_loop(0, n_blk, block, init)
        for h in range(ATT_HEADS):
            sl = slice(h * HEAD_DIM, (h + 1) * HEAD_DIM)
            o_ref[:, sl] = (acc_ref[h] / ls[h]).T.astype(o_ref.dtype)

    @pl.when(has_ties)
    def _():
        limit = tie_limit()
        attend(lambda k, r: ((k > thr) | ((k == thr) & (r <= limit))) & (r <= q_pos))

    @pl.when(jnp.logical_not(has_ties))
    def _():
        floor = jnp.maximum(thr, jnp.int32(INT_MIN + 1))
        attend(lambda k, r: k >= floor)


def _indexed_attention(iq, ik, iwT, aq, ak, avT, batch, seq):
    m = iq.shape[0]
    tq = _tiles(seq)["attn_q"]
    kb = avT.shape[3]
    nq = seq // tq
    topk = min(TOPK_MAX, seq // 4)
    q_spec = pl.BlockSpec((tq, GROUP_WIDTH), lambda b, i: (b * nq + i, 0))
    in_specs = [
        q_spec,
        pl.BlockSpec((seq, V7X_LANES), lambda b, i: (b, 0)),
        pl.BlockSpec((1, IDX_HEADS, tq), lambda b, i: (b, 0, i)),
        q_spec,
        pl.BlockSpec((seq, GROUP_WIDTH), lambda b, i: (b, 0)),
        pl.BlockSpec((1, seq // kb, GROUP_WIDTH, kb), lambda b, i: (b, 0, 0, 0)),
    ]
    scratch = [
        pltpu.VMEM((seq, tq), jnp.int32),
        pltpu.VMEM((ATT_HEADS, HEAD_DIM, tq), F32),
    ]
    return pl.pallas_call(
        functools.partial(_attn_kernel, topk), grid=(batch, nq),
        in_specs=in_specs, out_specs=q_spec,
        out_shape=jax.ShapeDtypeStruct((m, GROUP_WIDTH), BF16),
        scratch_shapes=scratch,
        compiler_params=_params(2), name="indexer_topk_attention",
    )(iq, ik, iwT, aq, ak, avT)


def _layer_norm(y, gain, bias):
    mu = jnp.mean(y, axis=-1, keepdims=True)
    cen = y - mu
    var = jnp.mean(cen * cen, axis=-1, keepdims=True)
    return cen * lax.rsqrt(var + LN_EPS) * gain + bias


def _outproj_kernel(alpha, ret_ref, att_ref, x_ref, w_ref, gain_ref, bias_ref, o_ref):
    width = ret_ref.shape[1]
    mix = jnp.dot(ret_ref[...], w_ref[0:width, :], preferred_element_type=F32)
    mix = mix + jnp.dot(att_ref[...], w_ref[width:, :], preferred_element_type=F32)
    o_ref[...] = _layer_norm(alpha * x_ref[...] + mix, gain_ref[...], bias_ref[...])


def _outproj(ret, att, x2d, w_out, gain, bias, alpha, seq):
    m, d = x2d.shape
    tm = _tiles(seq)["dense_rows"]
    rows = lambda width: pl.BlockSpec((tm, width), lambda i: (i, 0))
    return pl.pallas_call(
        functools.partial(_outproj_kernel, alpha), grid=(m // tm,),
        in_specs=[rows(GROUP_WIDTH), rows(GROUP_WIDTH), rows(d), _resident(w_out.shape),
                  _resident(gain.shape), _resident(bias.shape)],
        out_specs=rows(d), out_shape=jax.ShapeDtypeStruct((m, d), F32),
        compiler_params=_params(1), name="outproj_deepnorm",
    )(ret, att, x2d, w_out, gain, bias)


def _ffn_chunks(hidden):
    mxu_cols = 256
    assert hidden % mxu_cols == 0
    half = (hidden // mxu_cols + 1) // 2 * mxu_cols
    return [(0, half), (half, hidden)] if half < hidden else [(0, hidden)]


def _ffn_kernel(alpha, x_ref, wgu_ref, wd_ref, gain_ref, bias_ref, o_ref):
    hidden = wd_ref.shape[0]
    x = x_ref[...]
    xb = x.astype(BF16)
    acc = alpha * x
    for lo, hi in _ffn_chunks(hidden):
        gate = jnp.dot(xb, wgu_ref[:, lo:hi], preferred_element_type=F32)
        up = jnp.dot(xb, wgu_ref[:, hidden + lo:hidden + hi], preferred_element_type=F32)
        act = (gate * jax.nn.sigmoid(gate) * up).astype(BF16)
        acc = acc + jnp.dot(act, wd_ref[lo:hi, :], preferred_element_type=F32)
    o_ref[...] = _layer_norm(acc, gain_ref[...], bias_ref[...])


def _ffn(x2d, w_gate_up, w_down, gain, bias, alpha, seq):
    m, d = x2d.shape
    tm = _tiles(seq)["dense_rows"]
    rows = pl.BlockSpec((tm, d), lambda i: (i, 0))
    return pl.pallas_call(
        functools.partial(_ffn_kernel, alpha), grid=(m // tm,),
        in_specs=[rows, _resident(w_gate_up.shape), _resident(w_down.shape),
                  _resident(gain.shape), _resident(bias.shape)],
        out_specs=rows, out_shape=jax.ShapeDtypeStruct((m, d), F32),
        compiler_params=_params(1), name="ffn_deepnorm",
    )(x2d, w_gate_up, w_down, gain, bias)


def _rope_freq_tables():
    def table(dim):
        inv = ROPE_THETA ** (-jnp.arange(0, dim, 2, dtype=F32) / dim)
        return jnp.tile(inv, V7X_LANES // (dim // 2))[None, :]
    return table(HEAD_DIM), table(IDX_DIM)


def _split_w_in(w_in_l):
    g = GROUP_WIDTH
    w_main = w_in_l[:, :8 * g].astype(BF16)
    w_avT = w_in_l[:, 6 * g:7 * g].T.astype(BF16)
    ik = w_in_l[:, 8 * g:8 * g + IDX_DIM]
    iw = w_in_l[:, 8 * g + IDX_DIM:8 * g + IDX_DIM + IDX_HEADS]
    pad = jnp.zeros((w_in_l.shape[0], V7X_LANES - IDX_HEADS), w_in_l.dtype)
    w_last = jnp.concatenate([ik, ik, iw, pad], axis=1).astype(BF16)
    return w_main, w_avT, w_last


def kernel(x, positions, w_in, ret_gn_gain, w_out, ln_mix_gain, ln_mix_bias,
           w_gate_up, w_down, ln_ffn_gain, ln_ffn_bias):
    batch, seq, d = x.shape
    depth = w_in.shape[0]
    alpha = (2.0 * depth) ** 0.25
    f128, f64 = _rope_freq_tables()
    pos2d = positions.reshape(batch * seq, 1)
    x2d = x.reshape(batch * seq, d)
    for l in range(depth):
        w_main, w_avT, w_last = _split_w_in(w_in[l])
        rq, rk, rv, rg, aq, ak, avT, iq, ik, iwT = _inproj(
            x2d, pos2d, f128, f64, w_main, w_avT, w_last, batch, seq)
        ret = _retention(rq, rk, rv, rg, ret_gn_gain[l][None, :], batch, seq)
        att = _indexed_attention(iq, ik, iwT, aq, ak, avT, batch, seq)
        x2d = _outproj(ret, att, x2d, w_out[l].astype(BF16), ln_mix_gain[l][None, :],
                       ln_mix_bias[l][None, :], alpha, seq)
        x2d = _ffn(x2d, w_gate_up[l].astype(BF16), w_down[l].astype(BF16),
                   ln_ffn_gain[l][None, :], ln_ffn_bias[l][None, :], alpha, seq)
    return x2d.reshape(batch, seq, d)
```

```python
import functools
import math

import jax
import jax.numpy as jnp
from jax import lax
from jax.experimental import pallas as pl
from jax.experimental.pallas import tpu as pltpu

RET_HEADS = 4
ATT_HEADS = 4
HEAD_DIM = 128
IDX_HEADS = 8
IDX_DIM = 64
GROUP_WIDTH = 512
TOPK_MAX = 256
ROPE_THETA = 10000.0
RET_CHUNK = 128
LN_EPS = 1e-5

V7X_LANES = 128
BF16_TILE_ROWS = 16
I16_TILE_ROWS = 16
V7X_VMEM_BYTES = 64 * 1024 * 1024
VMEM_LIMIT_BYTES = V7X_VMEM_BYTES - 8 * 1024 * 1024

F32 = jnp.float32
BF16 = jnp.bfloat16
INT_MIN = -(2 ** 31)
I16_MIN, I16_MAX = -(2 ** 15), 2 ** 15 - 1
MASKED = -1e30


def _tiles(seq):
    t = {"proj_rows": min(512, seq), "attn_q": min(256, seq), "attn_keys": min(256, seq),
         "dense_rows": min(512, seq)}
    for v in t.values():
        assert seq % v == 0
    assert t["proj_rows"] % t["attn_keys"] == 0 and t["attn_q"] % t["attn_keys"] == 0
    return t


def _params(n_axes):
    return pltpu.CompilerParams(dimension_semantics=("arbitrary",) * n_axes,
                                vmem_limit_bytes=VMEM_LIMIT_BYTES)


def _resident(shape):
    zeros = (0,) * len(shape)
    return pl.BlockSpec(shape, lambda *_: zeros, pipeline_mode=pl.Buffered(1))


def _inproj_kernel(x_ref, pos_ref, f128_ref, f64_ref, wm_ref, wavT_ref, wl_ref,
                   rq_ref, rk_ref, rv_ref, rg_ref, aq_ref, ak_ref, avT_ref,
                   iq_ref, ik_ref, iwT_ref):
    xb = x_ref[...].astype(BF16)
    pos = pos_ref[...].astype(F32)
    lane = lax.broadcasted_iota(jnp.int32, (xb.shape[0], V7X_LANES), 1)

    ang = pos * f128_ref[...]
    cos128 = jnp.cos(ang)
    sin128 = jnp.sin(ang)
    sin128 = jnp.where(lane < HEAD_DIM // 2, -sin128, sin128)
    ang = pos * f64_ref[...]
    cos64 = jnp.cos(ang)
    sin64 = jnp.sin(ang)
    first64 = (lane % IDX_DIM) < IDX_DIM // 2
    sin64 = jnp.where(first64, -sin64, sin64)

    def proj(group):
        c0 = group * GROUP_WIDTH
        return jnp.dot(xb, wm_ref[:, c0:c0 + GROUP_WIDTH], preferred_element_type=F32)

    def rope128(y, scale):
        c, s = cos128 * scale, sin128 * scale
        return y * c + pltpu.roll(y, HEAD_DIM // 2, 1) * s

    def rope64(y, scale):
        c, s = cos64 * scale, sin64 * scale
        partner = jnp.where(first64, pltpu.roll(y, V7X_LANES - IDX_DIM // 2, 1),
                            pltpu.roll(y, IDX_DIM // 2, 1))
        return y * c + partner * s

    def store_roped(out_ref, y, fn, scale):
        for h in range(GROUP_WIDTH // V7X_LANES):
            sl = slice(h * V7X_LANES, (h + 1) * V7X_LANES)
            out_ref[:, sl] = fn(y[:, sl], scale).astype(out_ref.dtype)

    qk_scale = HEAD_DIM ** -0.5
    store_roped(rq_ref, proj(0), rope128, 1.0)
    store_roped(rk_ref, proj(1), rope128, qk_scale)
    rv_ref[...] = proj(2).astype(BF16)
    rg_ref[...] = proj(3).astype(BF16)
    store_roped(aq_ref, proj(4), rope128, qk_scale)
    store_roped(ak_ref, proj(5), rope128, 1.0)
    store_roped(iq_ref, proj(7), rope64, IDX_DIM ** -0.5)

    avT = lax.dot_general(wavT_ref[...], xb, (((1,), (1,)), ((), ())),
                          preferred_element_type=F32)
    kb = avT_ref.shape[3]
    for j in range(avT_ref.shape[1]):
        avT_ref[0, j] = avT[:, j * kb:(j + 1) * kb].astype(BF16)

    yl = jnp.dot(xb, wl_ref[...], preferred_element_type=F32)
    ik_ref[...] = rope64(yl[:, :V7X_LANES], 1.0).astype(BF16)
    iwT = yl[:, V7X_LANES:].T
    iwT_ref[0] = iwT[:IDX_HEADS, :] * (IDX_HEADS ** -0.5)


def _inproj(x2d, pos2d, f128, f64, w_main, w_avT, w_last, batch, seq):
    m, d = x2d.shape
    tm = _tiles(seq)["proj_rows"]
    kb = _tiles(seq)["attn_keys"]
    nj = seq // tm
    row = lambda b, j: (b * nj + j, 0)
    rows_spec = lambda width: pl.BlockSpec((tm, width), row)
    out_shape = [jax.ShapeDtypeStruct((m, GROUP_WIDTH), BF16)] * 6 + [
        jax.ShapeDtypeStruct((batch, seq // kb, GROUP_WIDTH, kb), BF16),
        jax.ShapeDtypeStruct((m, GROUP_WIDTH), BF16),
        jax.ShapeDtypeStruct((m, V7X_LANES), BF16),
        jax.ShapeDtypeStruct((batch, IDX_HEADS, seq), F32),
    ]
    out_specs = [rows_spec(GROUP_WIDTH)] * 6 + [
        pl.BlockSpec((1, tm // kb, GROUP_WIDTH, kb), lambda b, j: (b, j, 0, 0)),
        rows_spec(GROUP_WIDTH),
        rows_spec(V7X_LANES),
        pl.BlockSpec((1, IDX_HEADS, tm), lambda b, j: (b, 0, j)),
    ]
    in_specs = [rows_spec(d), rows_spec(1), _resident(f128.shape), _resident(f64.shape),
                _resident(w_main.shape), _resident(w_avT.shape), _resident(w_last.shape)]
    return pl.pallas_call(
        _inproj_kernel, grid=(batch, nj), in_specs=in_specs, out_specs=out_specs,
        out_shape=out_shape, compiler_params=_params(2), name="inproj_rope",
    )(x2d, pos2d, f128, f64, w_main, w_avT, w_last)


def _retention_kernel(q_ref, k_ref, v_ref, g_ref, gain_ref, o_ref):
    seq = q_ref.shape[0]
    c = RET_CHUNK
    n_chunks = seq // c
    ri = lax.broadcasted_iota(jnp.int32, (c, c), 0).astype(F32)
    ci = lax.broadcasted_iota(jnp.int32, (c, c), 1).astype(F32)
    rel = ri - ci
    pos_col = lax.broadcasted_iota(jnp.int32, (c, 1), 0).astype(F32)

    log_g = [math.log(1.0 - 2.0 ** (-5.0 - h)) for h in range(RET_HEADS)]
    decay_intra = [jnp.where(rel >= 0, jnp.exp(lg * jnp.maximum(rel, 0.0)), 0.0) for lg in log_g]
    decay_q = [jnp.exp(lg * (pos_col + 1.0)) for lg in log_g]
    decay_k = [jnp.exp(lg * (c - 1.0 - pos_col)) for lg in log_g]

    def chunk(n, states):
        rows = pl.ds(pl.multiple_of(n * c, c), c)
        new_states = []
        for h in range(RET_HEADS):
            sl = slice(h * HEAD_DIM, (h + 1) * HEAD_DIM)
            qc = q_ref[rows, sl]
            kc = k_ref[rows, sl]
            vc = v_ref[rows, sl]
            scores = lax.dot_general(qc, kc, (((1,), (1,)), ((), ())),
                                     preferred_element_type=F32) * decay_intra[h]
            inner = jnp.dot(scores.astype(BF16), vc, preferred_element_type=F32)
            cross = jnp.dot(qc, states[h].astype(BF16), preferred_element_type=F32) * decay_q[h]
            out = inner + cross
            kdT = (kc.astype(F32) * decay_k[h]).T.astype(BF16)
            kv = jnp.dot(kdT, vc, preferred_element_type=F32)
            new_states.append(math.exp(log_g[h] * c) * states[h] + kv)

            mu = jnp.mean(out, axis=-1, keepdims=True)
            cen = out - mu
            var = jnp.mean(cen * cen, axis=-1, keepdims=True)
            y = cen * lax.rsqrt(var + LN_EPS) * gain_ref[:, sl]
            g = g_ref[rows, sl].astype(F32)
            o_ref[rows, sl] = (g * jax.nn.sigmoid(g) * y).astype(o_ref.dtype)
        return tuple(new_states)

    zero = jnp.zeros((HEAD_DIM, HEAD_DIM), F32)
    lax.fori_loop(0, n_chunks, chunk, (zero,) * RET_HEADS)


def _retention(rq, rk, rv, rg, gain, batch, seq):
    m = rq.shape[0]
    blk = pl.BlockSpec((seq, GROUP_WIDTH), lambda b: (b, 0))
    return pl.pallas_call(
        _retention_kernel, grid=(batch,),
        in_specs=[blk, blk, blk, blk, _resident(gain.shape)], out_specs=blk,
        out_shape=jax.ShapeDtypeStruct((m, GROUP_WIDTH), BF16),
        compiler_params=_params(1), name="retention_gn_gate",
    )(rq, rk, rv, rg, gain)


def _sortable_key(score):
    score = jnp.where(score == 0.0, 0.0, score)
    bits = lax.bitcast_convert_type(score, jnp.int32)
    return jnp.where(bits < 0, bits ^ jnp.int32(0x7FFFFFFF), bits)


def _attn_kernel(topk, iq_ref, ik_ref, iwT_ref, aq_ref, ak_ref, avT_ref, o_ref,
                 key_ref, half_ref, acc_ref):
    tq = iq_ref.shape[0]
    kb = avT_ref.shape[3]
    qi = pl.program_id(1)
    n_blk = (qi * tq) // kb + 1
    kf = float(topk)
    nt = (((1,), (1,)), ((), ()))
    row = lax.broadcasted_iota(jnp.int32, (kb, tq), 0)
    q_pos = lax.broadcasted_iota(jnp.int32, (kb, tq), 1) + qi * tq

    def rows_of(j):
        return pl.ds(pl.multiple_of(j * kb, kb), kb)

    def index_block(j, carry):
        kk = ik_ref[rows_of(j), :]
        lane = lax.broadcasted_iota(jnp.int32, kk.shape, 1)
        k_lo = jnp.where(lane < IDX_DIM, kk, jnp.zeros_like(kk))
        k_hi = jnp.where(lane >= IDX_DIM, kk, jnp.zeros_like(kk))
        logits = []
        for pair in range(IDX_HEADS // 2):
            qp = iq_ref[:, pair * V7X_LANES:(pair + 1) * V7X_LANES]
            logits += [lax.dot_general(kmat, qp, nt, preferred_element_type=F32)
                       for kmat in (k_lo, k_hi)]
        score = jnp.zeros((kb, tq), F32)
        for h in range(IDX_HEADS):
            score = score + jnp.maximum(logits[h], 0.0) * iwT_ref[0, h:h + 1, :]
        key = jnp.where(row + j * kb <= q_pos, _sortable_key(score), jnp.int32(INT_MIN))
        key_ref[rows_of(j), :] = key
        half_ref[rows_of(j), :] = (key >> 16).astype(jnp.int16)
        return carry

    lax.fori_loop(0, n_blk, index_block, 0)

    def count(pred):
        def body(j, acc):
            ones = jnp.where(pred(key_ref[rows_of(j), :], row + j * kb), 1.0, 0.0)
            return acc + jnp.sum(ones.reshape(kb // 8, 8, tq), axis=0)
        acc = lax.fori_loop(0, n_blk, body, jnp.zeros((8, tq), F32))
        return jnp.sum(acc, axis=0, keepdims=True)

    def count_half_ge(cand):
        cand16 = cand.astype(jnp.int16)

        def body(j, acc):
            ones = jnp.where(half_ref[rows_of(j), :] >= cand16, jnp.int16(1), jnp.int16(0))
            parts = [ones[r:r + I16_TILE_ROWS, :] for r in range(0, kb, I16_TILE_ROWS)]
            while len(parts) > 1:
                parts = [a + b for a, b in zip(parts[0::2], parts[1::2])]
            return acc + parts[0].astype(jnp.int32)
        acc = lax.fori_loop(0, n_blk, body, jnp.zeros((I16_TILE_ROWS, tq), jnp.int32))
        return jnp.sum(acc.astype(F32), axis=0, keepdims=True)

    def kth_largest_half(k_wanted):
        prefix0 = jnp.where(count_half_ge(jnp.zeros((1, tq), jnp.int32)) >= k_wanted,
                            jnp.int32(0), jnp.int32(I16_MIN))

        def step(it, prefix):
            cand = prefix | jnp.left_shift(jnp.int32(1), 14 - it)
            return jnp.where(count_half_ge(cand) >= k_wanted, cand, prefix)
        return lax.fori_loop(0, 15, step, prefix0)

    hi = kth_largest_half(kf)
    above = jnp.where(hi == I16_MAX, 0.0, count_half_ge(jnp.minimum(hi + 1, I16_MAX)))

    def low_halves(j, carry):
        key = key_ref[rows_of(j), :]
        low = (key & 0xFFFF) + I16_MIN
        half_ref[rows_of(j), :] = jnp.where((key >> 16) == hi, low, I16_MIN).astype(jnp.int16)
        return carry

    lax.fori_loop(0, n_blk, low_halves, 0)
    lo = kth_largest_half(kf - above)
    thr = (hi << 16) | (lo - I16_MIN)
    need = kf - count(lambda k, r: k > thr)
    n_eq = count(lambda k, r: k == thr)

    idx_bits = max(1, (key_ref.shape[0] - 1).bit_length())

    def tie_limit():
        def step(it, prefix):
            cand = prefix + jnp.left_shift(jnp.int32(1), idx_bits - 1 - it)
            below = count(lambda k, r: (k == thr) & (r < cand))
            return jnp.where(below < need, cand, prefix)
        return lax.fori_loop(0, idx_bits, step, jnp.zeros((1, tq), jnp.int32))

    overfull = jnp.where((n_eq > need) & (thr > INT_MIN), 1.0, 0.0)
    has_ties = jnp.max(overfull) > 0.0

    def attend(is_selected):
        acc_ref[...] = jnp.zeros(acc_ref.shape, F32)
        ones_rows = jnp.ones((BF16_TILE_ROWS, kb), BF16)

        def block(j, carry):
            ms, ls = carry
            bias = jnp.where(is_selected(key_ref[rows_of(j), :], row + j * kb), 0.0, MASKED)
            heads = [slice(h * HEAD_DIM, (h + 1) * HEAD_DIM) for h in range(ATT_HEADS)]
            logits = [lax.dot_general(ak_ref[rows_of(j), sl], aq_ref[:, sl], nt,
                                      preferred_element_type=F32) for sl in heads]
            new_ms, probs, rescales = [], [], []
            for h in range(ATT_HEADS):
                s = logits[h] + bias
                m_new = jnp.maximum(ms[h], jnp.max(s, axis=0, keepdims=True))
                probs.append(jnp.exp(s - m_new).astype(BF16))
                rescales.append(jnp.exp(ms[h] - m_new))
                new_ms.append(m_new)
            new_ls = []
            for h, sl in enumerate(heads):
                v_aug = jnp.concatenate([avT_ref[0, j, sl, :], ones_rows], axis=0)
                pv = jnp.dot(v_aug, probs[h], preferred_element_type=F32)
                acc_ref[h] = rescales[h] * acc_ref[h] + pv[:HEAD_DIM]
                new_ls.append(rescales[h] * ls[h] + pv[HEAD_DIM:HEAD_DIM + 1])
            return tuple(new_ms), tuple(new_ls)

        init = ((jnp.full((1, tq), MASKED, F32),) * ATT_HEADS,
                (jnp.zeros((1, tq), F32),) * ATT_HEADS)
        _, ls = lax.fori_loop(0, n_blk, block, init)
        for h in range(ATT_HEADS):
            sl = slice(h * HEAD_DIM, (h + 1) * HEAD_DIM)
            o_ref[:, sl] = (acc_ref[h] / ls[h]).T.astype(o_ref.dtype)

    @pl.when(has_ties)
    def _():
        limit = tie_limit()
        attend(lambda k, r: ((k > thr) | ((k == thr) & (r <= limit))) & (r <= q_pos))

    @pl.when(jnp.logical_not(has_ties))
    def _():
        floor = jnp.maximum(thr, jnp.int32(INT_MIN + 1))
        attend(lambda k, r: k >= floor)


def _indexed_attention(iq, ik, iwT, aq, ak, avT, batch, seq):
    m = iq.shape[0]
    tq = _tiles(seq)["attn_q"]
    kb = avT.shape[3]
    nq = seq // tq
    topk = min(TOPK_MAX, seq // 4)
    q_spec = pl.BlockSpec((tq, GROUP_WIDTH), lambda b, i: (b * nq + i, 0))
    in_specs = [
        q_spec,
        pl.BlockSpec((seq, V7X_LANES), lambda b, i: (b, 0)),
        pl.BlockSpec((1, IDX_HEADS, tq), lambda b, i: (b, 0, i)),
        q_spec,
        pl.BlockSpec((seq, GROUP_WIDTH), lambda b, i: (b, 0)),
        pl.BlockSpec((1, seq // kb, GROUP_WIDTH, kb), lambda b, i: (b, 0, 0, 0)),
    ]
    scratch = [
        pltpu.VMEM((seq, tq), jnp.int32),
        pltpu.VMEM((seq, tq), jnp.int16),
        pltpu.VMEM((ATT_HEADS, HEAD_DIM, tq), F32),
    ]
    return pl.pallas_call(
        functools.partial(_attn_kernel, topk), grid=(batch, nq),
        in_specs=in_specs, out_specs=q_spec,
        out_shape=jax.ShapeDtypeStruct((m, GROUP_WIDTH), BF16),
        scratch_shapes=scratch,
        compiler_params=_params(2), name="indexer_topk_attention",
    )(iq, ik, iwT, aq, ak, avT)


def _layer_norm(y, gain, bias):
    mu = jnp.mean(y, axis=-1, keepdims=True)
    cen = y - mu
    var = jnp.mean(cen * cen, axis=-1, keepdims=True)
    return cen * lax.rsqrt(var + LN_EPS) * gain + bias


def _ffn_chunks(hidden):
    mxu_cols = 256
    assert hidden % mxu_cols == 0
    half = (hidden // mxu_cols + 1) // 2 * mxu_cols
    return [(0, half), (half, hidden)] if half < hidden else [(0, hidden)]


def _tail_kernel(alpha, ret_ref, att_ref, x_ref, wo_ref, g1_ref, b1_ref,
                 wgu_ref, wd_ref, g2_ref, b2_ref, o_ref):
    width = ret_ref.shape[1]
    mix = jnp.dot(ret_ref[...], wo_ref[0:width, :], preferred_element_type=F32)
    mix = mix + jnp.dot(att_ref[...], wo_ref[width:, :], preferred_element_type=F32)
    x = _layer_norm(alpha * x_ref[...] + mix, g1_ref[...], b1_ref[...])
    hidden = wd_ref.shape[0]
    xb = x.astype(BF16)
    acc = alpha * x
    for lo, hi in _ffn_chunks(hidden):
        gate = jnp.dot(xb, wgu_ref[:, lo:hi], preferred_element_type=F32)
        up = jnp.dot(xb, wgu_ref[:, hidden + lo:hidden + hi], preferred_element_type=F32)
        act = (gate * jax.nn.sigmoid(gate) * up).astype(BF16)
        acc = acc + jnp.dot(act, wd_ref[lo:hi, :], preferred_element_type=F32)
    o_ref[...] = _layer_norm(acc, g2_ref[...], b2_ref[...])


def _dense_tail(ret, att, x2d, w_out, g1, b1, w_gate_up, w_down, g2, b2, alpha, seq):
    m, d = x2d.shape
    tm = _tiles(seq)["dense_rows"]
    rows = lambda width: pl.BlockSpec((tm, width), lambda i: (i, 0))
    consts = [w_out, g1, b1, w_gate_up, w_down, g2, b2]
    return pl.pallas_call(
        functools.partial(_tail_kernel, alpha), grid=(m // tm,),
        in_specs=[rows(GROUP_WIDTH), rows(GROUP_WIDTH), rows(d)] + [_resident(c.shape) for c in consts],
        out_specs=rows(d), out_shape=jax.ShapeDtypeStruct((m, d), F32),
        compiler_params=_params(1), name="outproj_ffn_deepnorm",
    )(ret, att, x2d, *consts)


def _rope_freq_tables():
    def table(dim):
        inv = ROPE_THETA ** (-jnp.arange(0, dim, 2, dtype=F32) / dim)
        return jnp.tile(inv, V7X_LANES // (dim // 2))[None, :]
    return table(HEAD_DIM), table(IDX_DIM)


def _split_w_in(w_in_l):
    g = GROUP_WIDTH
    w_main = w_in_l[:, :8 * g].astype(BF16)
    w_avT = w_in_l[:, 6 * g:7 * g].T.astype(BF16)
    ik = w_in_l[:, 8 * g:8 * g + IDX_DIM]
    iw = w_in_l[:, 8 * g + IDX_DIM:8 * g + IDX_DIM + IDX_HEADS]
    pad = jnp.zeros((w_in_l.shape[0], V7X_LANES - IDX_HEADS), w_in_l.dtype)
    w_last = jnp.concatenate([ik, ik, iw, pad], axis=1).astype(BF16)
    return w_main, w_avT, w_last


def kernel(x, positions, w_in, ret_gn_gain, w_out, ln_mix_gain, ln_mix_bias,
           w_gate_up, w_down, ln_ffn_gain, ln_ffn_bias):
    batch, seq, d = x.shape
    depth = w_in.shape[0]
    alpha = (2.0 * depth) ** 0.25
    f128, f64 = _rope_freq_tables()
    pos2d = positions.reshape(batch * seq, 1)
    x2d = x.reshape(batch * seq, d)
    for l in range(depth):
        w_main, w_avT, w_last = _split_w_in(w_in[l])
        rq, rk, rv, rg, aq, ak, avT, iq, ik, iwT = _inproj(
            x2d, pos2d, f128, f64, w_main, w_avT, w_last, batch, seq)
        ret = _retention(rq, rk, rv, rg, ret_gn_gain[l][None, :], batch, seq)
        att = _indexed_attention(iq, ik, iwT, aq, ak, avT, batch, seq)
        x2d = _dense_tail(ret, att, x2d, w_out[l].astype(BF16), ln_mix_gain[l][None, :],
                          ln_mix_bias[l][None, :], w_gate_up[l].astype(BF16), w_down[l].astype(BF16),
                          ln_ffn_gain[l][None, :], ln_ffn_bias[l][None, :], alpha, seq)
    return x2d.reshape(batch, seq, d)
```

```python
import functools
import math

import jax
import jax.numpy as jnp
from jax import lax
from jax.experimental import pallas as pl
from jax.experimental.pallas import tpu as pltpu

RET_HEADS = 4
ATT_HEADS = 4
HEAD_DIM = 128
IDX_HEADS = 8
IDX_DIM = 64
GROUP_WIDTH = 512
TOPK_MAX = 256
ROPE_THETA = 10000.0
RET_CHUNK = 128
RET_STEP_CHUNKS = 4
LN_EPS = 1e-5

V7X_LANES = 128
BF16_TILE_ROWS = 16
I16_TILE_ROWS = 16
V7X_VMEM_BYTES = 64 * 1024 * 1024
VMEM_LIMIT_BYTES = V7X_VMEM_BYTES - 8 * 1024 * 1024

F32 = jnp.float32
BF16 = jnp.bfloat16
INT_MIN = -(2 ** 31)
I16_MIN, I16_MAX = -(2 ** 15), 2 ** 15 - 1
MASKED = -1e30


def _tiles(seq):
    t = {"proj_rows": min(512, seq), "attn_q": min(512, seq), "attn_keys": min(512, seq),
         "attn_scan_rows": min(256, seq), "dense_rows": min(512, seq)}
    for v in t.values():
        assert seq % v == 0
    assert t["proj_rows"] % t["attn_keys"] == 0 and t["attn_q"] % t["attn_keys"] == 0
    assert t["attn_keys"] % t["attn_scan_rows"] == 0
    return t


def _params(n_axes):
    return pltpu.CompilerParams(dimension_semantics=("arbitrary",) * n_axes,
                                vmem_limit_bytes=VMEM_LIMIT_BYTES)


def _resident(shape):
    zeros = (0,) * len(shape)
    return pl.BlockSpec(shape, lambda *_: zeros, pipeline_mode=pl.Buffered(1))


def _inproj_kernel(x_ref, pos_ref, freq_ref, wm_ref, wavT_ref, wl_ref,
                   rq_ref, rk_ref, rv_ref, rg_ref, aq_ref, ak_ref, avT_ref,
                   iq_ref, ik_ref, iwT_ref):
    xb = x_ref[...].astype(BF16)
    pos = pos_ref[...].astype(F32)
    lane = lax.broadcasted_iota(jnp.int32, (xb.shape[0], V7X_LANES), 1)

    ang = pos * freq_ref[...]
    cos, sin = jnp.cos(ang), jnp.sin(ang)
    low_lanes = lane < V7X_LANES // 2
    cos_swapped = pltpu.roll(cos, V7X_LANES // 2, 1)
    sin_swapped = pltpu.roll(sin, V7X_LANES // 2, 1)
    cos128 = jnp.where(low_lanes, cos, cos_swapped)
    sin128 = jnp.where(low_lanes, -sin, sin_swapped)
    first64 = (lane % IDX_DIM) < IDX_DIM // 2
    cos64 = jnp.where(low_lanes, cos_swapped, cos)
    sin64 = jnp.where(low_lanes, sin_swapped, sin)
    sin64 = jnp.where(first64, -sin64, sin64)

    def proj(group):
        c0 = group * GROUP_WIDTH
        return jnp.dot(xb, wm_ref[:, c0:c0 + GROUP_WIDTH], preferred_element_type=F32)

    def rope128(y, scale):
        c, s = cos128 * scale, sin128 * scale
        return y * c + pltpu.roll(y, HEAD_DIM // 2, 1) * s

    def rope64(y, scale):
        c, s = cos64 * scale, sin64 * scale
        partner = jnp.where(first64, pltpu.roll(y, V7X_LANES - IDX_DIM // 2, 1),
                            pltpu.roll(y, IDX_DIM // 2, 1))
        return y * c + partner * s

    def store_roped(out_ref, y, fn, scale):
        for h in range(GROUP_WIDTH // V7X_LANES):
            sl = slice(h * V7X_LANES, (h + 1) * V7X_LANES)
            out_ref[:, sl] = fn(y[:, sl], scale).astype(out_ref.dtype)

    qk_scale = HEAD_DIM ** -0.5
    store_roped(rq_ref, proj(0), rope128, 1.0)
    store_roped(rk_ref, proj(1), rope128, qk_scale)
    rv_ref[...] = proj(2).astype(BF16)
    rg_ref[...] = proj(3).astype(BF16)
    store_roped(aq_ref, proj(4), rope128, qk_scale * math.log2(math.e))
    store_roped(ak_ref, proj(5), rope128, 1.0)
    store_roped(iq_ref, proj(7), rope64, IDX_DIM ** -0.5)

    avT = lax.dot_general(wavT_ref[...], xb, (((1,), (1,)), ((), ())),
                          preferred_element_type=F32)
    kb = avT_ref.shape[3]
    for j in range(avT_ref.shape[1]):
        avT_ref[0, j] = avT[:, j * kb:(j + 1) * kb].astype(BF16)

    yl = jnp.dot(xb, wl_ref[...], preferred_element_type=F32)
    ik_ref[...] = rope64(yl[:, :V7X_LANES], 1.0).astype(BF16)
    iwT = yl[:, V7X_LANES:].T
    iwT_ref[0] = iwT[:IDX_HEADS, :] * (IDX_HEADS ** -0.5)


def _inproj(x2d, pos2d, freqs, w_main, w_avT, w_last, batch, seq):
    m, d = x2d.shape
    tm = _tiles(seq)["proj_rows"]
    kb = _tiles(seq)["attn_keys"]
    nj = seq // tm
    row = lambda b, j: (b * nj + j, 0)
    rows_spec = lambda width: pl.BlockSpec((tm, width), row)
    out_shape = [jax.ShapeDtypeStruct((m, GROUP_WIDTH), BF16)] * 6 + [
        jax.ShapeDtypeStruct((batch, seq // kb, GROUP_WIDTH, kb), BF16),
        jax.ShapeDtypeStruct((m, GROUP_WIDTH), BF16),
        jax.ShapeDtypeStruct((m, V7X_LANES), BF16),
        jax.ShapeDtypeStruct((batch, IDX_HEADS, seq), F32),
    ]
    out_specs = [rows_spec(GROUP_WIDTH)] * 6 + [
        pl.BlockSpec((1, tm // kb, GROUP_WIDTH, kb), lambda b, j: (b, j, 0, 0)),
        rows_spec(GROUP_WIDTH),
        rows_spec(V7X_LANES),
        pl.BlockSpec((1, IDX_HEADS, tm), lambda b, j: (b, 0, j)),
    ]
    in_specs = [rows_spec(d), rows_spec(1), _resident(freqs.shape),
                _resident(w_main.shape), _resident(w_avT.shape), _resident(w_last.shape)]
    return pl.pallas_call(
        _inproj_kernel, grid=(batch, nj), in_specs=in_specs, out_specs=out_specs,
        out_shape=out_shape, compiler_params=_params(2), name="inproj_rope",
    )(x2d, pos2d, freqs, w_main, w_avT, w_last)


def _retention_kernel(q_ref, k_ref, v_ref, g_ref, gain_ref, o_ref):
    seq = q_ref.shape[0]
    c = RET_CHUNK
    n_chunks = seq // c
    ri = lax.broadcasted_iota(jnp.int32, (c, c), 0).astype(F32)
    ci = lax.broadcasted_iota(jnp.int32, (c, c), 1).astype(F32)
    rel = ri - ci
    pos_col = lax.broadcasted_iota(jnp.int32, (c, 1), 0).astype(F32)

    log_g = [math.log(1.0 - 2.0 ** (-5.0 - h)) for h in range(RET_HEADS)]
    decay_intra = [jnp.where(rel >= 0, jnp.exp(lg * jnp.maximum(rel, 0.0)), 0.0) for lg in log_g]
    decay_q = [jnp.exp(lg * (pos_col + 1.0)) for lg in log_g]
    decay_k = [jnp.exp(lg * (c - 1.0 - pos_col)) for lg in log_g]

    nt = (((1,), (1,)), ((), ()))
    heads = [slice(h * HEAD_DIM, (h + 1) * HEAD_DIM) for h in range(RET_HEADS)]
    chunk_decay = [math.exp(lg * c) for lg in log_g]

    def step(n, states):
        rows = [pl.ds(pl.multiple_of((n * RET_STEP_CHUNKS + i) * c, c), c)
                for i in range(RET_STEP_CHUNKS)]
        q = [[q_ref[r, sl] for r in rows] for sl in heads]
        k = [[k_ref[r, sl] for r in rows] for sl in heads]
        v = [[v_ref[r, sl] for r in rows] for sl in heads]
        scores = [[lax.dot_general(q[h][i], k[h][i], nt, preferred_element_type=F32)
                   for i in range(RET_STEP_CHUNKS)] for h in range(RET_HEADS)]
        kv = [[jnp.dot((k[h][i].astype(F32) * decay_k[h]).T.astype(BF16), v[h][i],
                       preferred_element_type=F32)
               for i in range(RET_STEP_CHUNKS)] for h in range(RET_HEADS)]
        seen = []
        for h in range(RET_HEADS):
            chain = [states[h]]
            for i in range(RET_STEP_CHUNKS):
                chain.append(chunk_decay[h] * chain[-1] + kv[h][i])
            seen.append(chain)
        weights = [[(scores[h][i] * decay_intra[h]).astype(BF16)
                    for i in range(RET_STEP_CHUNKS)] for h in range(RET_HEADS)]
        outs = [[jnp.dot(weights[h][i], v[h][i], preferred_element_type=F32)
                 + jnp.dot(q[h][i], seen[h][i].astype(BF16), preferred_element_type=F32) * decay_q[h]
                 for i in range(RET_STEP_CHUNKS)] for h in range(RET_HEADS)]
        for h, sl in enumerate(heads):
            for i, r in enumerate(rows):
                out = outs[h][i]
                mu = jnp.mean(out, axis=-1, keepdims=True)
                cen = out - mu
                var = jnp.mean(cen * cen, axis=-1, keepdims=True)
                y = cen * lax.rsqrt(var + LN_EPS) * gain_ref[:, sl]
                g = g_ref[r, sl].astype(F32)
                o_ref[r, sl] = (g * jax.nn.sigmoid(g) * y).astype(o_ref.dtype)
        return tuple(chain[-1] for chain in seen)

    assert n_chunks % RET_STEP_CHUNKS == 0
    zero = jnp.zeros((HEAD_DIM, HEAD_DIM), F32)
    lax.fori_loop(0, n_chunks // RET_STEP_CHUNKS, step, (zero,) * RET_HEADS)


def _retention(rq, rk, rv, rg, gain, batch, seq):
    m = rq.shape[0]
    blk = pl.BlockSpec((seq, GROUP_WIDTH), lambda b: (b, 0))
    return pl.pallas_call(
        _retention_kernel, grid=(batch,),
        in_specs=[blk, blk, blk, blk, _resident(gain.shape)], out_specs=blk,
        out_shape=jax.ShapeDtypeStruct((m, GROUP_WIDTH), BF16),
        compiler_params=_params(1), name="retention_gn_gate",
    )(rq, rk, rv, rg, gain)


def _sortable_key(score):
    score = jnp.where(score == 0.0, 0.0, score)
    bits = lax.bitcast_convert_type(score, jnp.int32)
    return jnp.where(bits < 0, bits ^ jnp.int32(0x7FFFFFFF), bits)


def _attn_kernel(topk, rb, iq_ref, ik_ref, iwT_ref, aq_ref, ak_ref, avT_ref, o_ref,
                 key_ref, half_ref, acc_ref):
    tq = iq_ref.shape[0]
    kb = avT_ref.shape[3]
    qi = pl.program_id(1)
    n_blk = ((qi + 1) * tq) // kb
    n_scan = n_blk * (kb // rb)
    kf = float(topk)
    nt = (((1,), (1,)), ((), ()))
    row = lax.broadcasted_iota(jnp.int32, (kb, tq), 0)
    scan_row = lax.broadcasted_iota(jnp.int32, (rb, tq), 0)
    q_pos = lax.broadcasted_iota(jnp.int32, (kb, tq), 1) + qi * tq

    def rows_of(j, size=kb):
        return pl.ds(pl.multiple_of(j * size, size), size)

    def index_block(j, carry):
        kk = ik_ref[rows_of(j), :]
        lane = lax.broadcasted_iota(jnp.int32, kk.shape, 1)
        k_lo = jnp.where(lane < IDX_DIM, kk, jnp.zeros_like(kk))
        k_hi = jnp.where(lane >= IDX_DIM, kk, jnp.zeros_like(kk))
        logits = []
        for pair in range(IDX_HEADS // 2):
            qp = iq_ref[:, pair * V7X_LANES:(pair + 1) * V7X_LANES]
            logits += [lax.dot_general(kmat, qp, nt, preferred_element_type=F32)
                       for kmat in (k_lo, k_hi)]
        score = jnp.zeros((kb, tq), F32)
        for h in range(IDX_HEADS):
            score = score + jnp.maximum(logits[h], 0.0) * iwT_ref[0, h:h + 1, :]
        key = jnp.where(row + j * kb <= q_pos, _sortable_key(score), jnp.int32(INT_MIN))
        key_ref[rows_of(j), :] = key
        half_ref[rows_of(j), :] = (key >> 16).astype(jnp.int16)
        return carry

    lax.fori_loop(0, n_blk, index_block, 0)

    def count(pred):
        def body(j, acc):
            ones = jnp.where(pred(key_ref[rows_of(j, rb), :], scan_row + j * rb), 1.0, 0.0)
            return acc + jnp.sum(ones.reshape(rb // 8, 8, tq), axis=0)
        acc = lax.fori_loop(0, n_scan, body, jnp.zeros((8, tq), F32))
        return jnp.sum(acc, axis=0, keepdims=True)

    def count_half_ge(cand):
        cand16 = cand.astype(jnp.int16)

        def body(j, acc):
            ones = jnp.where(half_ref[rows_of(j, rb), :] >= cand16, jnp.int16(1), jnp.int16(0))
            parts = [ones[r:r + I16_TILE_ROWS, :] for r in range(0, rb, I16_TILE_ROWS)]
            while len(parts) > 1:
                parts = [a + b for a, b in zip(parts[0::2], parts[1::2])]
            return acc + parts[0].astype(jnp.int32)
        acc = lax.fori_loop(0, n_scan, body, jnp.zeros((I16_TILE_ROWS, tq), jnp.int32))
        return jnp.sum(acc.astype(F32), axis=0, keepdims=True)

    def kth_largest_half(k_wanted):
        prefix0 = jnp.where(count_half_ge(jnp.zeros((1, tq), jnp.int32)) >= k_wanted,
                            jnp.int32(0), jnp.int32(I16_MIN))

        def step(it, prefix):
            cand = prefix | jnp.left_shift(jnp.int32(1), 14 - it)
            return jnp.where(count_half_ge(cand) >= k_wanted, cand, prefix)
        return lax.fori_loop(0, 15, step, prefix0)

    hi = kth_largest_half(kf)
    above = jnp.where(hi == I16_MAX, 0.0, count_half_ge(jnp.minimum(hi + 1, I16_MAX)))

    def low_halves(j, carry):
        key = key_ref[rows_of(j, rb), :]
        low = (key & 0xFFFF) + I16_MIN
        half_ref[rows_of(j, rb), :] = jnp.where((key >> 16) == hi, low, I16_MIN).astype(jnp.int16)
        return carry

    lax.fori_loop(0, n_scan, low_halves, 0)
    lo = kth_largest_half(kf - above)
    thr = (hi << 16) | (lo - I16_MIN)
    need = kf - count(lambda k, r: k > thr)
    n_eq = count(lambda k, r: k == thr)

    idx_bits = max(1, (key_ref.shape[0] - 1).bit_length())

    def tie_limit():
        def fill(j, carry):
            tied = key_ref[rows_of(j, rb), :] == thr
            half_ref[rows_of(j, rb), :] = jnp.where(tied, scan_row + j * rb, I16_MAX).astype(jnp.int16)
            return carry

        lax.fori_loop(0, n_scan, fill, 0)
        scanned = (n_scan * rb).astype(F32)

        def step(it, prefix):
            cand = prefix + jnp.left_shift(jnp.int32(1), idx_bits - 1 - it)
            below = scanned - count_half_ge(cand)
            return jnp.where(below < need, cand, prefix)
        return lax.fori_loop(0, idx_bits, step, jnp.zeros((1, tq), jnp.int32))

    overfull = jnp.where((n_eq > need) & (thr > INT_MIN), 1.0, 0.0)
    has_ties = jnp.max(overfull) > 0.0

    def attend(is_selected):
        acc_ref[...] = jnp.zeros(acc_ref.shape, F32)
        ones_rows = jnp.ones((BF16_TILE_ROWS, kb), BF16)

        def block(j, carry):
            ms, ls = carry
            bias = jnp.where(is_selected(key_ref[rows_of(j), :], row + j * kb), 0.0, MASKED)
            heads = [slice(h * HEAD_DIM, (h + 1) * HEAD_DIM) for h in range(ATT_HEADS)]
            logits = [lax.dot_general(ak_ref[rows_of(j), sl], aq_ref[:, sl], nt,
                                      preferred_element_type=F32) for sl in heads]
            new_ms, probs, rescales = [], [], []
            for h in range(ATT_HEADS):
                s = logits[h] + bias
                m_new = jnp.maximum(ms[h], jnp.max(s, axis=0, keepdims=True))
                probs.append(jnp.exp2(s - m_new).astype(BF16))
                rescales.append(jnp.exp2(ms[h] - m_new))
                new_ms.append(m_new)
            new_ls = []
            for h, sl in enumerate(heads):
                v_aug = jnp.concatenate([avT_ref[0, j, sl, :], ones_rows], axis=0)
                pv = jnp.dot(v_aug, probs[h], preferred_element_type=F32)
                acc_ref[h] = rescales[h] * acc_ref[h] + pv[:HEAD_DIM]
                new_ls.append(rescales[h] * ls[h] + pv[HEAD_DIM:HEAD_DIM + 1])
            return tuple(new_ms), tuple(new_ls)

        init = ((jnp.full((1, tq), MASKED, F32),) * ATT_HEADS,
                (jnp.zeros((1, tq), F32),) * ATT_HEADS)
        _, ls = lax.fori_loop(0, n_blk, block, init)
        for h in range(ATT_HEADS):
            sl = slice(h * HEAD_DIM, (h + 1) * HEAD_DIM)
            o_ref[:, sl] = (acc_ref[h] / ls[h]).T.astype(o_ref.dtype)

    @pl.when(has_ties)
    def _():
        limit = tie_limit()
        attend(lambda k, r: ((k > thr) | ((k == thr) & (r <= limit))) & (r <= q_pos))

    @pl.when(jnp.logical_not(has_ties))
    def _():
        floor = jnp.maximum(thr, jnp.int32(INT_MIN + 1))
        attend(lambda k, r: k >= floor)


def _indexed_attention(iq, ik, iwT, aq, ak, avT, batch, seq):
    m = iq.shape[0]
    tq = _tiles(seq)["attn_q"]
    kb = avT.shape[3]
    nq = seq // tq
    topk = min(TOPK_MAX, seq // 4)
    q_spec = pl.BlockSpec((tq, GROUP_WIDTH), lambda b, i: (b * nq + i, 0))
    in_specs = [
        q_spec,
        pl.BlockSpec((seq, V7X_LANES), lambda b, i: (b, 0)),
        pl.BlockSpec((1, IDX_HEADS, tq), lambda b, i: (b, 0, i)),
        q_spec,
        pl.BlockSpec((seq, GROUP_WIDTH), lambda b, i: (b, 0)),
        pl.BlockSpec((1, seq // kb, GROUP_WIDTH, kb), lambda b, i: (b, 0, 0, 0)),
    ]
    scratch = [
        pltpu.VMEM((seq, tq), jnp.int32),
        pltpu.VMEM((seq, tq), jnp.int16),
        pltpu.VMEM((ATT_HEADS, HEAD_DIM, tq), F32),
    ]
    return pl.pallas_call(
        functools.partial(_attn_kernel, topk, _tiles(seq)["attn_scan_rows"]), grid=(batch, nq),
        in_specs=in_specs, out_specs=q_spec,
        out_shape=jax.ShapeDtypeStruct((m, GROUP_WIDTH), BF16),
        scratch_shapes=scratch,
        compiler_params=_params(2), name="indexer_topk_attention",
    )(iq, ik, iwT, aq, ak, avT)


def _layer_norm(y, gain, bias):
    mu = jnp.mean(y, axis=-1, keepdims=True)
    cen = y - mu
    var = jnp.mean(cen * cen, axis=-1, keepdims=True)
    return cen * lax.rsqrt(var + LN_EPS) * gain + bias


def _ffn_chunks(hidden):
    mxu_cols = 256
    assert hidden % mxu_cols == 0
    half = (hidden // mxu_cols + 1) // 2 * mxu_cols
    return [(0, half), (half, hidden)] if half < hidden else [(0, hidden)]


def _tail_kernel(alpha, ret_ref, att_ref, x_ref, wo_ref, g1_ref, b1_ref,
                 wgu_ref, wd_ref, g2_ref, b2_ref, o_ref):
    width = ret_ref.shape[1]
    mix = jnp.dot(ret_ref[...], wo_ref[0:width, :], preferred_element_type=F32)
    mix = mix + jnp.dot(att_ref[...], wo_ref[width:, :], preferred_element_type=F32)
    x = _layer_norm(alpha * x_ref[...] + mix, g1_ref[...], b1_ref[...])
    hidden = wd_ref.shape[0]
    xb = x.astype(BF16)
    acc = alpha * x
    for lo, hi in _ffn_chunks(hidden):
        gate = jnp.dot(xb, wgu_ref[:, lo:hi], preferred_element_type=F32)
        up = jnp.dot(xb, wgu_ref[:, hidden + lo:hidden + hi], preferred_element_type=F32)
        act = (gate * jax.nn.sigmoid(gate) * up).astype(BF16)
        acc = acc + jnp.dot(act, wd_ref[lo:hi, :], preferred_element_type=F32)
    o_ref[...] = _layer_norm(acc, g2_ref[...], b2_ref[...])


def _dense_tail(ret, att, x2d, w_out, g1, b1, w_gate_up, w_down, g2, b2, alpha, seq):
    m, d = x2d.shape
    tm = _tiles(seq)["dense_rows"]
    rows = lambda width: pl.BlockSpec((tm, width), lambda i: (i, 0))
    consts = [w_out, g1, b1, w_gate_up, w_down, g2, b2]
    return pl.pallas_call(
        functools.partial(_tail_kernel, alpha), grid=(m // tm,),
        in_specs=[rows(GROUP_WIDTH), rows(GROUP_WIDTH), rows(d)] + [_resident(c.shape) for c in consts],
        out_specs=rows(d), out_shape=jax.ShapeDtypeStruct((m, d), F32),
        compiler_params=_params(1), name="outproj_ffn_deepnorm",
    )(ret, att, x2d, *consts)


def _rope_frequencies():
    def inv_freq(dim):
        return ROPE_THETA ** (-jnp.arange(0, dim, 2, dtype=F32) / dim)
    return jnp.concatenate([inv_freq(HEAD_DIM), inv_freq(IDX_DIM), inv_freq(IDX_DIM)])[None, :]


def _split_w_in(w_in_l):
    g = GROUP_WIDTH
    w_main = w_in_l[:, :8 * g].astype(BF16)
    w_avT = w_in_l[:, 6 * g:7 * g].T.astype(BF16)
    ik = w_in_l[:, 8 * g:8 * g + IDX_DIM]
    iw = w_in_l[:, 8 * g + IDX_DIM:8 * g + IDX_DIM + IDX_HEADS]
    pad = jnp.zeros((w_in_l.shape[0], V7X_LANES - IDX_HEADS), w_in_l.dtype)
    w_last = jnp.concatenate([ik, ik, iw, pad], axis=1).astype(BF16)
    return w_main, w_avT, w_last


def kernel(x, positions, w_in, ret_gn_gain, w_out, ln_mix_gain, ln_mix_bias,
           w_gate_up, w_down, ln_ffn_gain, ln_ffn_bias):
    batch, seq, d = x.shape
    depth = w_in.shape[0]
    alpha = (2.0 * depth) ** 0.25
    freqs = _rope_frequencies()
    pos2d = positions.reshape(batch * seq, 1)
    x2d = x.reshape(batch * seq, d)
    for l in range(depth):
        w_main, w_avT, w_last = _split_w_in(w_in[l])
        rq, rk, rv, rg, aq, ak, avT, iq, ik, iwT = _inproj(
            x2d, pos2d, freqs, w_main, w_avT, w_last, batch, seq)
        ret = _retention(rq, rk, rv, rg, ret_gn_gain[l][None, :], batch, seq)
        att = _indexed_attention(iq, ik, iwT, aq, ak, avT, batch, seq)
        x2d = _dense_tail(ret, att, x2d, w_out[l].astype(BF16), ln_mix_gain[l][None, :],
                          ln_mix_bias[l][None, :], w_gate_up[l].astype(BF16), w_down[l].astype(BF16),
                          ln_ffn_gain[l][None, :], ln_ffn_bias[l][None, :], alpha, seq)
    return x2d.reshape(batch, seq, d)
```

```python
import functools
import math

import jax
import jax.numpy as jnp
from jax import lax
from jax.experimental import pallas as pl
from jax.experimental.pallas import tpu as pltpu

RET_HEADS = 4
ATT_HEADS = 4
HEAD_DIM = 128
IDX_HEADS = 8
IDX_DIM = 64
GROUP_WIDTH = 512
TOPK_MAX = 256
ROPE_THETA = 10000.0
RET_CHUNK = 128
RET_STEP_CHUNKS = 4
TAIL_ROW_PARTS = 2
LN_EPS = 1e-5

V7X_LANES = 128
BF16_TILE_ROWS = 16
I16_TILE_ROWS = 16
V7X_VMEM_BYTES = 64 * 1024 * 1024
VMEM_LIMIT_BYTES = V7X_VMEM_BYTES - 8 * 1024 * 1024

F32 = jnp.float32
BF16 = jnp.bfloat16
INT_MIN = -(2 ** 31)
I16_MIN, I16_MAX = -(2 ** 15), 2 ** 15 - 1
MASKED = -1e30


def _tiles(seq):
    t = {"proj_rows": min(512, seq), "attn_q": min(512, seq), "attn_keys": min(512, seq),
         "attn_scan_rows": min(256, seq), "dense_rows": min(512, seq)}
    for v in t.values():
        assert seq % v == 0
    assert t["proj_rows"] % t["attn_keys"] == 0 and t["attn_q"] % t["attn_keys"] == 0
    assert t["attn_keys"] % t["attn_scan_rows"] == 0
    return t


def _params(n_axes):
    return pltpu.CompilerParams(dimension_semantics=("arbitrary",) * n_axes,
                                vmem_limit_bytes=VMEM_LIMIT_BYTES)


def _resident(shape):
    zeros = (0,) * len(shape)
    return pl.BlockSpec(shape, lambda *_: zeros, pipeline_mode=pl.Buffered(1))


def _inproj_kernel(x_ref, pos_ref, freq_ref, wm_ref, wavT_ref, wl_ref,
                   rq_ref, rk_ref, rv_ref, rg_ref, aq_ref, ak_ref, avT_ref,
                   iq_ref, ik_ref, iwT_ref):
    xb = x_ref[...].astype(BF16)
    pos = pos_ref[...].astype(F32)
    lane = lax.broadcasted_iota(jnp.int32, (xb.shape[0], V7X_LANES), 1)

    ang = pos * freq_ref[...]
    cos, sin = jnp.cos(ang), jnp.sin(ang)
    low_lanes = lane < V7X_LANES // 2
    cos_swapped = pltpu.roll(cos, V7X_LANES // 2, 1)
    sin_swapped = pltpu.roll(sin, V7X_LANES // 2, 1)
    cos128 = jnp.where(low_lanes, cos, cos_swapped)
    sin128 = jnp.where(low_lanes, -sin, sin_swapped)
    first64 = (lane % IDX_DIM) < IDX_DIM // 2
    cos64 = jnp.where(low_lanes, cos_swapped, cos)
    sin64 = jnp.where(low_lanes, sin_swapped, sin)
    sin64 = jnp.where(first64, -sin64, sin64)

    def proj(group):
        c0 = group * GROUP_WIDTH
        return jnp.dot(xb, wm_ref[:, c0:c0 + GROUP_WIDTH], preferred_element_type=F32)

    def rope128(y, scale):
        c, s = cos128 * scale, sin128 * scale
        return y * c + pltpu.roll(y, HEAD_DIM // 2, 1) * s

    def rope64(y, scale):
        c, s = cos64 * scale, sin64 * scale
        partner = jnp.where(first64, pltpu.roll(y, V7X_LANES - IDX_DIM // 2, 1),
                            pltpu.roll(y, IDX_DIM // 2, 1))
        return y * c + partner * s

    def store_roped(out_ref, y, fn, scale):
        for h in range(GROUP_WIDTH // V7X_LANES):
            sl = slice(h * V7X_LANES, (h + 1) * V7X_LANES)
            out_ref[:, sl] = fn(y[:, sl], scale).astype(out_ref.dtype)

    qk_scale = HEAD_DIM ** -0.5
    store_roped(rq_ref, proj(0), rope128, 1.0)
    store_roped(rk_ref, proj(1), rope128, qk_scale)
    rv_ref[...] = proj(2).astype(BF16)
    rg_ref[...] = proj(3).astype(BF16)
    store_roped(aq_ref, proj(4), rope128, qk_scale * math.log2(math.e))
    store_roped(ak_ref, proj(5), rope128, 1.0)
    store_roped(iq_ref, proj(7), rope64, IDX_DIM ** -0.5)

    avT = lax.dot_general(wavT_ref[...], xb, (((1,), (1,)), ((), ())),
                          preferred_element_type=F32)
    kb = avT_ref.shape[3]
    for j in range(avT_ref.shape[1]):
        avT_ref[0, j] = avT[:, j * kb:(j + 1) * kb].astype(BF16)

    yl = jnp.dot(xb, wl_ref[...], preferred_element_type=F32)
    ik_ref[...] = rope64(yl[:, :V7X_LANES], 1.0).astype(BF16)
    iwT = yl[:, V7X_LANES:].T
    iwT_ref[0] = iwT[:IDX_HEADS, :] * (IDX_HEADS ** -0.5)


def _inproj(x2d, pos2d, freqs, w_main, w_avT, w_last, batch, seq):
    m, d = x2d.shape
    tm = _tiles(seq)["proj_rows"]
    kb = _tiles(seq)["attn_keys"]
    nj = seq // tm
    row = lambda b, j: (b * nj + j, 0)
    rows_spec = lambda width: pl.BlockSpec((tm, width), row)
    out_shape = [jax.ShapeDtypeStruct((m, GROUP_WIDTH), BF16)] * 6 + [
        jax.ShapeDtypeStruct((batch, seq // kb, GROUP_WIDTH, kb), BF16),
        jax.ShapeDtypeStruct((m, GROUP_WIDTH), BF16),
        jax.ShapeDtypeStruct((m, V7X_LANES), BF16),
        jax.ShapeDtypeStruct((batch, IDX_HEADS, seq), F32),
    ]
    out_specs = [rows_spec(GROUP_WIDTH)] * 6 + [
        pl.BlockSpec((1, tm // kb, GROUP_WIDTH, kb), lambda b, j: (b, j, 0, 0)),
        rows_spec(GROUP_WIDTH),
        rows_spec(V7X_LANES),
        pl.BlockSpec((1, IDX_HEADS, tm), lambda b, j: (b, 0, j)),
    ]
    in_specs = [rows_spec(d), rows_spec(1), _resident(freqs.shape),
                _resident(w_main.shape), _resident(w_avT.shape), _resident(w_last.shape)]
    return pl.pallas_call(
        _inproj_kernel, grid=(batch, nj), in_specs=in_specs, out_specs=out_specs,
        out_shape=out_shape, compiler_params=_params(2), name="inproj_rope",
    )(x2d, pos2d, freqs, w_main, w_avT, w_last)


def _retention_kernel(q_ref, k_ref, v_ref, g_ref, gain_ref, o_ref):
    seq = q_ref.shape[0]
    c = RET_CHUNK
    n_chunks = seq // c
    ri = lax.broadcasted_iota(jnp.int32, (c, c), 0).astype(F32)
    ci = lax.broadcasted_iota(jnp.int32, (c, c), 1).astype(F32)
    rel = ri - ci
    pos_col = lax.broadcasted_iota(jnp.int32, (c, 1), 0).astype(F32)

    log_g = [math.log(1.0 - 2.0 ** (-5.0 - h)) for h in range(RET_HEADS)]
    decay_intra = [jnp.where(rel >= 0, jnp.exp(lg * jnp.maximum(rel, 0.0)), 0.0) for lg in log_g]
    decay_q = [jnp.exp(lg * (pos_col + 1.0)) for lg in log_g]
    decay_k = [jnp.exp(lg * (c - 1.0 - pos_col)) for lg in log_g]

    nt = (((1,), (1,)), ((), ()))
    heads = [slice(h * HEAD_DIM, (h + 1) * HEAD_DIM) for h in range(RET_HEADS)]
    chunk_decay = [math.exp(lg * c) for lg in log_g]

    def step(n, states):
        rows = [pl.ds(pl.multiple_of((n * RET_STEP_CHUNKS + i) * c, c), c)
                for i in range(RET_STEP_CHUNKS)]
        q = [[q_ref[r, sl] for r in rows] for sl in heads]
        k = [[k_ref[r, sl] for r in rows] for sl in heads]
        v = [[v_ref[r, sl] for r in rows] for sl in heads]
        scores = [[lax.dot_general(q[h][i], k[h][i], nt, preferred_element_type=F32)
                   for i in range(RET_STEP_CHUNKS)] for h in range(RET_HEADS)]
        kv = [[jnp.dot((k[h][i].astype(F32) * decay_k[h]).T.astype(BF16), v[h][i],
                       preferred_element_type=F32)
               for i in range(RET_STEP_CHUNKS)] for h in range(RET_HEADS)]
        seen = []
        for h in range(RET_HEADS):
            chain = [states[h]]
            for i in range(RET_STEP_CHUNKS):
                chain.append(chunk_decay[h] * chain[-1] + kv[h][i])
            seen.append(chain)
        weights = [[(scores[h][i] * decay_intra[h]).astype(BF16)
                    for i in range(RET_STEP_CHUNKS)] for h in range(RET_HEADS)]
        outs = [[jnp.dot(weights[h][i], v[h][i], preferred_element_type=F32)
                 + jnp.dot(q[h][i], seen[h][i].astype(BF16), preferred_element_type=F32) * decay_q[h]
                 for i in range(RET_STEP_CHUNKS)] for h in range(RET_HEADS)]
        for h, sl in enumerate(heads):
            for i, r in enumerate(rows):
                out = outs[h][i]
                mu = jnp.mean(out, axis=-1, keepdims=True)
                cen = out - mu
                var = jnp.mean(cen * cen, axis=-1, keepdims=True)
                y = cen * lax.rsqrt(var + LN_EPS) * gain_ref[:, sl]
                g = g_ref[r, sl].astype(F32)
                o_ref[r, sl] = (g * jax.nn.sigmoid(g) * y).astype(o_ref.dtype)
        return tuple(chain[-1] for chain in seen)

    assert n_chunks % RET_STEP_CHUNKS == 0
    zero = jnp.zeros((HEAD_DIM, HEAD_DIM), F32)
    lax.fori_loop(0, n_chunks // RET_STEP_CHUNKS, step, (zero,) * RET_HEADS)


def _retention(rq, rk, rv, rg, gain, batch, seq):
    m = rq.shape[0]
    blk = pl.BlockSpec((seq, GROUP_WIDTH), lambda b: (b, 0))
    return pl.pallas_call(
        _retention_kernel, grid=(batch,),
        in_specs=[blk, blk, blk, blk, _resident(gain.shape)], out_specs=blk,
        out_shape=jax.ShapeDtypeStruct((m, GROUP_WIDTH), BF16),
        compiler_params=_params(1), name="retention_gn_gate",
    )(rq, rk, rv, rg, gain)


def _sortable_key(score):
    score = jnp.where(score == 0.0, 0.0, score)
    bits = lax.bitcast_convert_type(score, jnp.int32)
    return jnp.where(bits < 0, bits ^ jnp.int32(0x7FFFFFFF), bits)


def _attn_kernel(topk, rb, iq_ref, ik_ref, iwT_ref, aq_ref, ak_ref, avT_ref, o_ref,
                 key_ref, half_ref, acc_ref):
    tq = iq_ref.shape[0]
    kb = avT_ref.shape[3]
    qi = pl.program_id(1)
    n_blk = ((qi + 1) * tq) // kb
    n_scan = n_blk * (kb // rb)
    kf = float(topk)
    nt = (((1,), (1,)), ((), ()))
    row = lax.broadcasted_iota(jnp.int32, (kb, tq), 0)
    scan_row = lax.broadcasted_iota(jnp.int32, (rb, tq), 0)
    q_pos = lax.broadcasted_iota(jnp.int32, (kb, tq), 1) + qi * tq

    def rows_of(j, size=kb):
        return pl.ds(pl.multiple_of(j * size, size), size)

    def index_block(j, carry, on_diagonal):
        kk = ik_ref[rows_of(j), :]
        lane = lax.broadcasted_iota(jnp.int32, kk.shape, 1)
        k_lo = jnp.where(lane < IDX_DIM, kk, jnp.zeros_like(kk))
        k_hi = jnp.where(lane >= IDX_DIM, kk, jnp.zeros_like(kk))
        logits = []
        for pair in range(IDX_HEADS // 2):
            qp = iq_ref[:, pair * V7X_LANES:(pair + 1) * V7X_LANES]
            logits += [lax.dot_general(kmat, qp, nt, preferred_element_type=F32)
                       for kmat in (k_lo, k_hi)]
        score = jnp.zeros((kb, tq), F32)
        for h in range(IDX_HEADS):
            score = score + jnp.maximum(logits[h], 0.0) * iwT_ref[0, h:h + 1, :]
        key = _sortable_key(score)
        if on_diagonal:
            key = jnp.where(row + j * kb <= q_pos, key, jnp.int32(INT_MIN))
        key_ref[rows_of(j), :] = key
        half_ref[rows_of(j), :] = (key >> 16).astype(jnp.int16)
        return carry

    n_below = (qi * tq + 1) // kb
    lax.fori_loop(0, n_below, functools.partial(index_block, on_diagonal=False), 0)
    lax.fori_loop(n_below, n_blk, functools.partial(index_block, on_diagonal=True), 0)

    def count(pred):
        def body(j, acc):
            ones = jnp.where(pred(key_ref[rows_of(j, rb), :], scan_row + j * rb), 1.0, 0.0)
            return acc + jnp.sum(ones.reshape(rb // 8, 8, tq), axis=0)
        acc = lax.fori_loop(0, n_scan, body, jnp.zeros((8, tq), F32))
        return jnp.sum(acc, axis=0, keepdims=True)

    def count_half_ge(cand):
        cand16 = cand.astype(jnp.int16)

        def body(j, acc):
            ones = jnp.where(half_ref[rows_of(j, rb), :] >= cand16, jnp.int16(1), jnp.int16(0))
            parts = [ones[r:r + I16_TILE_ROWS, :] for r in range(0, rb, I16_TILE_ROWS)]
            while len(parts) > 1:
                parts = [a + b for a, b in zip(parts[0::2], parts[1::2])]
            return acc + parts[0].astype(jnp.int32)
        acc = lax.fori_loop(0, n_scan, body, jnp.zeros((I16_TILE_ROWS, tq), jnp.int32))
        return jnp.sum(acc.astype(F32), axis=0, keepdims=True)

    def kth_largest_half(k_wanted):
        prefix0 = jnp.where(count_half_ge(jnp.zeros((1, tq), jnp.int32)) >= k_wanted,
                            jnp.int32(0), jnp.int32(I16_MIN))

        def step(it, prefix):
            cand = prefix | jnp.left_shift(jnp.int32(1), 14 - it)
            return jnp.where(count_half_ge(cand) >= k_wanted, cand, prefix)
        return lax.fori_loop(0, 15, step, prefix0)

    hi = kth_largest_half(kf)
    above = jnp.where(hi == I16_MAX, 0.0, count_half_ge(jnp.minimum(hi + 1, I16_MAX)))

    def low_halves(j, carry):
        key = key_ref[rows_of(j, rb), :]
        low = (key & 0xFFFF) + I16_MIN
        half_ref[rows_of(j, rb), :] = jnp.where((key >> 16) == hi, low, I16_MIN).astype(jnp.int16)
        return carry

    lax.fori_loop(0, n_scan, low_halves, 0)
    lo = kth_largest_half(kf - above)
    thr = (hi << 16) | (lo - I16_MIN)
    need = kf - count(lambda k, r: k > thr)
    n_eq = count(lambda k, r: k == thr)

    idx_bits = max(1, (key_ref.shape[0] - 1).bit_length())

    def tie_limit():
        def fill(j, carry):
            tied = key_ref[rows_of(j, rb), :] == thr
            half_ref[rows_of(j, rb), :] = jnp.where(tied, scan_row + j * rb, I16_MAX).astype(jnp.int16)
            return carry

        lax.fori_loop(0, n_scan, fill, 0)
        scanned = (n_scan * rb).astype(F32)

        def step(it, prefix):
            cand = prefix + jnp.left_shift(jnp.int32(1), idx_bits - 1 - it)
            below = scanned - count_half_ge(cand)
            return jnp.where(below < need, cand, prefix)
        return lax.fori_loop(0, idx_bits, step, jnp.zeros((1, tq), jnp.int32))

    overfull = jnp.where((n_eq > need) & (thr > INT_MIN), 1.0, 0.0)
    has_ties = jnp.max(overfull) > 0.0

    @pl.when(has_ties)
    def _():
        limit = tie_limit()

        def demote(j, carry):
            key = key_ref[rows_of(j, rb), :]
            drop = (key == thr) & (scan_row + j * rb > limit)
            key_ref[rows_of(j, rb), :] = jnp.where(drop, jnp.int32(INT_MIN), key)
            return carry

        lax.fori_loop(0, n_scan, demote, 0)

    floor = jnp.maximum(thr, jnp.int32(INT_MIN + 1))

    def attend():
        acc_ref[...] = jnp.zeros(acc_ref.shape, F32)
        ones_rows = jnp.ones((BF16_TILE_ROWS, kb), BF16)

        def block(j, carry):
            ms, ls = carry
            bias = jnp.where(key_ref[rows_of(j), :] >= floor, 0.0, MASKED)
            heads = [slice(h * HEAD_DIM, (h + 1) * HEAD_DIM) for h in range(ATT_HEADS)]
            logits = [lax.dot_general(ak_ref[rows_of(j), sl], aq_ref[:, sl], nt,
                                      preferred_element_type=F32) for sl in heads]
            new_ms, probs, rescales = [], [], []
            for h in range(ATT_HEADS):
                s = logits[h] + bias
                m_new = jnp.maximum(ms[h], jnp.max(s, axis=0, keepdims=True))
                probs.append(jnp.exp2(s - m_new).astype(BF16))
                rescales.append(jnp.exp2(ms[h] - m_new))
                new_ms.append(m_new)
            new_ls = []
            for h, sl in enumerate(heads):
                v_aug = jnp.concatenate([avT_ref[0, j, sl, :], ones_rows], axis=0)
                pv = jnp.dot(v_aug, probs[h], preferred_element_type=F32)
                acc_ref[h] = rescales[h] * acc_ref[h] + pv[:HEAD_DIM]
                new_ls.append(rescales[h] * ls[h] + pv[HEAD_DIM:HEAD_DIM + 1])
            return tuple(new_ms), tuple(new_ls)

        init = ((jnp.full((1, tq), MASKED, F32),) * ATT_HEADS,
                (jnp.zeros((1, tq), F32),) * ATT_HEADS)
        _, ls = lax.fori_loop(0, n_blk, block, init)
        for h in range(ATT_HEADS):
            sl = slice(h * HEAD_DIM, (h + 1) * HEAD_DIM)
            o_ref[:, sl] = (acc_ref[h] / ls[h]).T.astype(o_ref.dtype)

    attend()


def _indexed_attention(iq, ik, iwT, aq, ak, avT, batch, seq):
    m = iq.shape[0]
    tq = _tiles(seq)["attn_q"]
    kb = avT.shape[3]
    nq = seq // tq
    topk = min(TOPK_MAX, seq // 4)
    q_spec = pl.BlockSpec((tq, GROUP_WIDTH), lambda b, i: (b * nq + i, 0))
    in_specs = [
        q_spec,
        pl.BlockSpec((seq, V7X_LANES), lambda b, i: (b, 0)),
        pl.BlockSpec((1, IDX_HEADS, tq), lambda b, i: (b, 0, i)),
        q_spec,
        pl.BlockSpec((seq, GROUP_WIDTH), lambda b, i: (b, 0)),
        pl.BlockSpec((1, seq // kb, GROUP_WIDTH, kb), lambda b, i: (b, 0, 0, 0)),
    ]
    scratch = [
        pltpu.VMEM((seq, tq), jnp.int32),
        pltpu.VMEM((seq, tq), jnp.int16),
        pltpu.VMEM((ATT_HEADS, HEAD_DIM, tq), F32),
    ]
    return pl.pallas_call(
        functools.partial(_attn_kernel, topk, _tiles(seq)["attn_scan_rows"]), grid=(batch, nq),
        in_specs=in_specs, out_specs=q_spec,
        out_shape=jax.ShapeDtypeStruct((m, GROUP_WIDTH), BF16),
        scratch_shapes=scratch,
        compiler_params=_params(2), name="indexer_topk_attention",
    )(iq, ik, iwT, aq, ak, avT)


def _layer_norm(y, gain, bias):
    mu = jnp.mean(y, axis=-1, keepdims=True)
    cen = y - mu
    var = jnp.mean(cen * cen, axis=-1, keepdims=True)
    return cen * lax.rsqrt(var + LN_EPS) * gain + bias


def _ffn_chunks(hidden):
    mxu_cols = 256
    assert hidden % mxu_cols == 0
    half = (hidden // mxu_cols + 1) // 2 * mxu_cols
    return [(0, half), (half, hidden)] if half < hidden else [(0, hidden)]


def _tail_kernel(alpha, ret_ref, att_ref, x_ref, wo_ref, g1_ref, b1_ref,
                 wgu_ref, wd_ref, g2_ref, b2_ref, o_ref):
    width = ret_ref.shape[1]
    hidden = wd_ref.shape[0]
    part = ret_ref.shape[0] // TAIL_ROW_PARTS
    halves = [slice(i * part, (i + 1) * part) for i in range(TAIL_ROW_PARTS)]

    def mixed(rows):
        mix = jnp.dot(ret_ref[rows, :], wo_ref[0:width, :], preferred_element_type=F32)
        mix = mix + jnp.dot(att_ref[rows, :], wo_ref[width:, :], preferred_element_type=F32)
        return _layer_norm(alpha * x_ref[rows, :] + mix, g1_ref[...], b1_ref[...])

    def ffn_chunk(xb, lo, hi):
        gate = jnp.dot(xb, wgu_ref[:, lo:hi], preferred_element_type=F32)
        up = jnp.dot(xb, wgu_ref[:, hidden + lo:hidden + hi], preferred_element_type=F32)
        act = (gate * jax.nn.sigmoid(gate) * up).astype(BF16)
        return jnp.dot(act, wd_ref[lo:hi, :], preferred_element_type=F32)

    xs = [mixed(rows) for rows in halves]
    xbs = [x.astype(BF16) for x in xs]
    accs = [alpha * x for x in xs]
    for lo, hi in _ffn_chunks(hidden):
        accs = [acc + ffn_chunk(xb, lo, hi) for acc, xb in zip(accs, xbs)]
    for rows, acc in zip(halves, accs):
        o_ref[rows, :] = _layer_norm(acc, g2_ref[...], b2_ref[...])


def _dense_tail(ret, att, x2d, w_out, g1, b1, w_gate_up, w_down, g2, b2, alpha, seq):
    m, d = x2d.shape
    tm = _tiles(seq)["dense_rows"]
    rows = lambda width: pl.BlockSpec((tm, width), lambda i: (i, 0))
    consts = [w_out, g1, b1, w_gate_up, w_down, g2, b2]
    return pl.pallas_call(
        functools.partial(_tail_kernel, alpha), grid=(m // tm,),
        in_specs=[rows(GROUP_WIDTH), rows(GROUP_WIDTH), rows(d)] + [_resident(c.shape) for c in consts],
        out_specs=rows(d), out_shape=jax.ShapeDtypeStruct((m, d), F32),
        compiler_params=_params(1), name="outproj_ffn_deepnorm",
    )(ret, att, x2d, *consts)


def _rope_frequencies():
    def inv_freq(dim):
        return ROPE_THETA ** (-jnp.arange(0, dim, 2, dtype=F32) / dim)
    return jnp.concatenate([inv_freq(HEAD_DIM), inv_freq(IDX_DIM), inv_freq(IDX_DIM)])[None, :]


def _split_w_in(w_in_l):
    g = GROUP_WIDTH
    w_main = w_in_l[:, :8 * g].astype(BF16)
    w_avT = w_in_l[:, 6 * g:7 * g].T.astype(BF16)
    ik = w_in_l[:, 8 * g:8 * g + IDX_DIM]
    iw = w_in_l[:, 8 * g + IDX_DIM:8 * g + IDX_DIM + IDX_HEADS]
    pad = jnp.zeros((w_in_l.shape[0], V7X_LANES - IDX_HEADS), w_in_l.dtype)
    w_last = jnp.concatenate([ik, ik, iw, pad], axis=1).astype(BF16)
    return w_main, w_avT, w_last


def kernel(x, positions, w_in, ret_gn_gain, w_out, ln_mix_gain, ln_mix_bias,
           w_gate_up, w_down, ln_ffn_gain, ln_ffn_bias):
    batch, seq, d = x.shape
    depth = w_in.shape[0]
    alpha = (2.0 * depth) ** 0.25
    freqs = _rope_frequencies()
    pos2d = positions.reshape(batch * seq, 1)
    x2d = x.reshape(batch * seq, d)
    for l in range(depth):
        w_main, w_avT, w_last = _split_w_in(w_in[l])
        rq, rk, rv, rg, aq, ak, avT, iq, ik, iwT = _inproj(
            x2d, pos2d, freqs, w_main, w_avT, w_last, batch, seq)
        ret = _retention(rq, rk, rv, rg, ret_gn_gain[l][None, :], batch, seq)
        att = _indexed_attention(iq, ik, iwT, aq, ak, avT, batch, seq)
        x2d = _dense_tail(ret, att, x2d, w_out[l].astype(BF16), ln_mix_gain[l][None, :],
                          ln_mix_bias[l][None, :], w_gate_up[l].astype(BF16), w_down[l].astype(BF16),
                          ln_ffn_gain[l][None, :], ln_ffn_bias[l][None, :], alpha, seq)
    return x2d.reshape(batch, seq, d)
```

```python
import functools
import math

import jax
import jax.numpy as jnp
from jax import lax
from jax.experimental import pallas as pl
from jax.experimental.pallas import tpu as pltpu

RET_HEADS = 4
ATT_HEADS = 4
HEAD_DIM = 128
IDX_HEADS = 8
IDX_DIM = 64
GROUP_WIDTH = 512
TOPK_MAX = 256
ROPE_THETA = 10000.0
RET_CHUNK = 128
RET_STEP_CHUNKS = 4
TAIL_ROW_PARTS = 2
LN_EPS = 1e-5

V7X_LANES = 128
BF16_TILE_ROWS = 16
I16_TILE_ROWS = 16
V7X_VMEM_BYTES = 64 * 1024 * 1024
VMEM_LIMIT_BYTES = V7X_VMEM_BYTES - 8 * 1024 * 1024

F32 = jnp.float32
BF16 = jnp.bfloat16
INT_MIN = -(2 ** 31)
I16_MIN, I16_MAX = -(2 ** 15), 2 ** 15 - 1
MASKED = -1e30


def _tiles(seq):
    t = {"proj_rows": min(512, seq), "attn_q": min(512, seq), "attn_keys": min(512, seq),
         "attn_scan_rows": min(256, seq), "dense_rows": min(512, seq)}
    for v in t.values():
        assert seq % v == 0
    assert t["proj_rows"] % t["attn_keys"] == 0 and t["attn_q"] % t["attn_keys"] == 0
    assert t["attn_keys"] % t["attn_scan_rows"] == 0
    return t


def _params(n_axes):
    return pltpu.CompilerParams(dimension_semantics=("arbitrary",) * n_axes,
                                vmem_limit_bytes=VMEM_LIMIT_BYTES)


def _resident(shape):
    zeros = (0,) * len(shape)
    return pl.BlockSpec(shape, lambda *_: zeros, pipeline_mode=pl.Buffered(1))


def _inproj_kernel(x_ref, pos_ref, freq_ref, wm_ref, wavT_ref, wl_ref,
                   rq_ref, rk_ref, rv_ref, rg_ref, aq_ref, ak_ref, avT_ref,
                   iq_ref, ik_ref, iwT_ref):
    xb = x_ref[...].astype(BF16)
    pos = pos_ref[...].astype(F32)
    lane = lax.broadcasted_iota(jnp.int32, (xb.shape[0], V7X_LANES), 1)

    ang = pos * freq_ref[...]
    cos, sin = jnp.cos(ang), jnp.sin(ang)
    low_lanes = lane < V7X_LANES // 2
    cos_swapped = pltpu.roll(cos, V7X_LANES // 2, 1)
    sin_swapped = pltpu.roll(sin, V7X_LANES // 2, 1)
    cos128 = jnp.where(low_lanes, cos, cos_swapped)
    sin128 = jnp.where(low_lanes, -sin, sin_swapped)
    first64 = (lane % IDX_DIM) < IDX_DIM // 2
    cos64 = jnp.where(low_lanes, cos_swapped, cos)
    sin64 = jnp.where(low_lanes, sin_swapped, sin)
    sin64 = jnp.where(first64, -sin64, sin64)

    def proj(group):
        c0 = group * GROUP_WIDTH
        return jnp.dot(xb, wm_ref[:, c0:c0 + GROUP_WIDTH], preferred_element_type=F32)

    def rope128(y, scale):
        c, s = cos128 * scale, sin128 * scale
        return y * c + pltpu.roll(y, HEAD_DIM // 2, 1) * s

    def rope64(y, scale):
        c, s = cos64 * scale, sin64 * scale
        partner = jnp.where(first64, pltpu.roll(y, V7X_LANES - IDX_DIM // 2, 1),
                            pltpu.roll(y, IDX_DIM // 2, 1))
        return y * c + partner * s

    def store_roped(out_ref, y, fn, scale):
        for h in range(GROUP_WIDTH // V7X_LANES):
            sl = slice(h * V7X_LANES, (h + 1) * V7X_LANES)
            out_ref[:, sl] = fn(y[:, sl], scale).astype(out_ref.dtype)

    qk_scale = HEAD_DIM ** -0.5
    store_roped(rq_ref, proj(0), rope128, 1.0)
    store_roped(rk_ref, proj(1), rope128, qk_scale)
    rv_ref[...] = proj(2).astype(BF16)
    rg_ref[...] = proj(3).astype(BF16)
    store_roped(aq_ref, proj(4), rope128, qk_scale * math.log2(math.e))
    store_roped(ak_ref, proj(5), rope128, 1.0)
    store_roped(iq_ref, proj(7), rope64, IDX_DIM ** -0.5)

    avT = lax.dot_general(wavT_ref[...], xb, (((1,), (1,)), ((), ())),
                          preferred_element_type=F32)
    kb = avT_ref.shape[3]
    for j in range(avT_ref.shape[1]):
        avT_ref[0, j] = avT[:, j * kb:(j + 1) * kb].astype(BF16)

    yl = jnp.dot(xb, wl_ref[...], preferred_element_type=F32)
    ik_ref[...] = rope64(yl[:, :V7X_LANES], 1.0).astype(BF16)
    iwT = yl[:, V7X_LANES:].T
    iwT_ref[0] = iwT[:IDX_HEADS, :] * (IDX_HEADS ** -0.5)


def _inproj(x2d, pos2d, freqs, w_main, w_avT, w_last, batch, seq):
    m, d = x2d.shape
    tm = _tiles(seq)["proj_rows"]
    kb = _tiles(seq)["attn_keys"]
    nj = seq // tm
    row = lambda b, j: (b * nj + j, 0)
    rows_spec = lambda width: pl.BlockSpec((tm, width), row)
    out_shape = [jax.ShapeDtypeStruct((m, GROUP_WIDTH), BF16)] * 6 + [
        jax.ShapeDtypeStruct((batch, seq // kb, GROUP_WIDTH, kb), BF16),
        jax.ShapeDtypeStruct((m, GROUP_WIDTH), BF16),
        jax.ShapeDtypeStruct((m, V7X_LANES), BF16),
        jax.ShapeDtypeStruct((batch, IDX_HEADS, seq), F32),
    ]
    out_specs = [rows_spec(GROUP_WIDTH)] * 6 + [
        pl.BlockSpec((1, tm // kb, GROUP_WIDTH, kb), lambda b, j: (b, j, 0, 0)),
        rows_spec(GROUP_WIDTH),
        rows_spec(V7X_LANES),
        pl.BlockSpec((1, IDX_HEADS, tm), lambda b, j: (b, 0, j)),
    ]
    in_specs = [rows_spec(d), rows_spec(1), _resident(freqs.shape),
                _resident(w_main.shape), _resident(w_avT.shape), _resident(w_last.shape)]
    return pl.pallas_call(
        _inproj_kernel, grid=(batch, nj), in_specs=in_specs, out_specs=out_specs,
        out_shape=out_shape, compiler_params=_params(2), name="inproj_rope",
    )(x2d, pos2d, freqs, w_main, w_avT, w_last)


def _retention_kernel(q_ref, k_ref, v_ref, g_ref, gain_ref, o_ref):
    seq = q_ref.shape[0]
    c = RET_CHUNK
    n_chunks = seq // c
    ri = lax.broadcasted_iota(jnp.int32, (c, c), 0).astype(F32)
    ci = lax.broadcasted_iota(jnp.int32, (c, c), 1).astype(F32)
    rel = ri - ci
    pos_col = lax.broadcasted_iota(jnp.int32, (c, 1), 0).astype(F32)

    log_g = [math.log(1.0 - 2.0 ** (-5.0 - h)) for h in range(RET_HEADS)]
    decay_intra = [jnp.where(rel >= 0, jnp.exp(lg * jnp.maximum(rel, 0.0)), 0.0) for lg in log_g]
    decay_q = [jnp.exp(lg * (pos_col + 1.0)) for lg in log_g]
    decay_k = [jnp.exp(lg * (c - 1.0 - pos_col)) for lg in log_g]

    nt = (((1,), (1,)), ((), ()))
    heads = [slice(h * HEAD_DIM, (h + 1) * HEAD_DIM) for h in range(RET_HEADS)]
    chunk_decay = [math.exp(lg * c) for lg in log_g]

    def step(n, states):
        rows = [pl.ds(pl.multiple_of((n * RET_STEP_CHUNKS + i) * c, c), c)
                for i in range(RET_STEP_CHUNKS)]
        q = [[q_ref[r, sl] for r in rows] for sl in heads]
        k = [[k_ref[r, sl] for r in rows] for sl in heads]
        v = [[v_ref[r, sl] for r in rows] for sl in heads]
        scores = [[lax.dot_general(q[h][i], k[h][i], nt, preferred_element_type=F32)
                   for i in range(RET_STEP_CHUNKS)] for h in range(RET_HEADS)]
        kv = [[jnp.dot((k[h][i].astype(F32) * decay_k[h]).T.astype(BF16), v[h][i],
                       preferred_element_type=F32)
               for i in range(RET_STEP_CHUNKS)] for h in range(RET_HEADS)]
        seen = []
        for h in range(RET_HEADS):
            chain = [states[h]]
            for i in range(RET_STEP_CHUNKS):
                chain.append(chunk_decay[h] * chain[-1] + kv[h][i])
            seen.append(chain)
        weights = [[(scores[h][i] * decay_intra[h]).astype(BF16)
                    for i in range(RET_STEP_CHUNKS)] for h in range(RET_HEADS)]
        outs = [[jnp.dot(weights[h][i], v[h][i], preferred_element_type=F32)
                 + jnp.dot(q[h][i], seen[h][i].astype(BF16), preferred_element_type=F32) * decay_q[h]
                 for i in range(RET_STEP_CHUNKS)] for h in range(RET_HEADS)]
        for h, sl in enumerate(heads):
            for i, r in enumerate(rows):
                out = outs[h][i]
                mu = jnp.mean(out, axis=-1, keepdims=True)
                cen = out - mu
                var = jnp.mean(cen * cen, axis=-1, keepdims=True)
                y = cen * lax.rsqrt(var + LN_EPS) * gain_ref[:, sl]
                g = g_ref[r, sl].astype(F32)
                o_ref[r, sl] = (g * jax.nn.sigmoid(g) * y).astype(o_ref.dtype)
        return tuple(chain[-1] for chain in seen)

    assert n_chunks % RET_STEP_CHUNKS == 0
    zero = jnp.zeros((HEAD_DIM, HEAD_DIM), F32)
    lax.fori_loop(0, n_chunks // RET_STEP_CHUNKS, step, (zero,) * RET_HEADS)


def _retention(rq, rk, rv, rg, gain, batch, seq):
    m = rq.shape[0]
    blk = pl.BlockSpec((seq, GROUP_WIDTH), lambda b: (b, 0))
    return pl.pallas_call(
        _retention_kernel, grid=(batch,),
        in_specs=[blk, blk, blk, blk, _resident(gain.shape)], out_specs=blk,
        out_shape=jax.ShapeDtypeStruct((m, GROUP_WIDTH), BF16),
        compiler_params=_params(1), name="retention_gn_gate",
    )(rq, rk, rv, rg, gain)


def _sortable_key(score):
    score = jnp.where(score == 0.0, 0.0, score)
    bits = lax.bitcast_convert_type(score, jnp.int32)
    return jnp.where(bits < 0, bits ^ jnp.int32(0x7FFFFFFF), bits)


def _attn_kernel(topk, rb, iq_ref, ik_ref, iwT_ref, aq_ref, ak_ref, avT_ref, o_ref,
                 key_ref, half_ref, acc_ref):
    tq = iq_ref.shape[0]
    kb = avT_ref.shape[3]
    qi = pl.program_id(1)
    n_blk = ((qi + 1) * tq) // kb
    n_scan = n_blk * (kb // rb)
    kf = float(topk)
    nt = (((1,), (1,)), ((), ()))
    scan_row = lax.broadcasted_iota(jnp.int32, (rb, tq), 0)

    def rows_of(j, size=kb):
        return pl.ds(pl.multiple_of(j * size, size), size)

    def index_tile(j, r0, n_rows, l0, on_diagonal):
        rows = pl.ds(pl.multiple_of(j * kb + r0, rb), n_rows)
        kk = ik_ref[rows, :]
        lane = lax.broadcasted_iota(jnp.int32, kk.shape, 1)
        k_lo = jnp.where(lane < IDX_DIM, kk, jnp.zeros_like(kk))
        k_hi = jnp.where(lane >= IDX_DIM, kk, jnp.zeros_like(kk))
        logits = []
        for pair in range(IDX_HEADS // 2):
            qp = iq_ref[l0:, pair * V7X_LANES:(pair + 1) * V7X_LANES]
            logits += [lax.dot_general(kmat, qp, nt, preferred_element_type=F32)
                       for kmat in (k_lo, k_hi)]
        score = jnp.zeros((n_rows, tq - l0), F32)
        for h in range(IDX_HEADS):
            score = score + jnp.maximum(logits[h], 0.0) * iwT_ref[0, h:h + 1, l0:]
        key = _sortable_key(score)
        if on_diagonal:
            key_pos = lax.broadcasted_iota(jnp.int32, key.shape, 0) + (j * kb + r0)
            query_pos = lax.broadcasted_iota(jnp.int32, key.shape, 1) + (qi * tq + l0)
            key = jnp.where(key_pos <= query_pos, key, jnp.int32(INT_MIN))
        key_ref[rows, l0:] = key
        half_ref[rows, l0:] = (key >> 16).astype(jnp.int16)
        if l0:
            key_ref[rows, :l0] = jnp.full((n_rows, l0), INT_MIN, jnp.int32)
            half_ref[rows, :l0] = jnp.full((n_rows, l0), I16_MIN, jnp.int16)

    def index_below(j, carry):
        index_tile(j, 0, kb, 0, on_diagonal=False)
        return carry

    assert tq == kb
    lax.fori_loop(0, n_blk - 1, index_below, 0)
    for r in range(0, kb, rb):
        index_tile(n_blk - 1, r, rb, r, on_diagonal=True)

    dead = tq - rb

    def count_half_ge(cand):
        cand16 = cand.astype(jnp.int16)

        def partial_count(values, cand_part):
            ones = jnp.where(values >= cand_part, jnp.int16(1), jnp.int16(0))
            parts = [ones[r:r + I16_TILE_ROWS, :] for r in range(0, rb, I16_TILE_ROWS)]
            while len(parts) > 1:
                parts = [a + b for a, b in zip(parts[0::2], parts[1::2])]
            return parts[0].astype(jnp.int32)

        def body(j, acc):
            return acc + partial_count(half_ref[rows_of(j, rb), :], cand16)
        acc = lax.fori_loop(0, n_scan - 1, body, jnp.zeros((I16_TILE_ROWS, tq), jnp.int32))
        last = partial_count(half_ref[rows_of(n_scan - 1, rb), dead:], cand16[:, dead:])
        acc = jnp.concatenate([acc[:, :dead], acc[:, dead:] + last], axis=1)
        return jnp.sum(acc.astype(F32), axis=0, keepdims=True)

    def kth_largest_half(k_wanted):
        prefix0 = jnp.where(count_half_ge(jnp.zeros((1, tq), jnp.int32)) >= k_wanted,
                            jnp.int32(0), jnp.int32(I16_MIN))

        def step(it, prefix):
            cand = prefix | jnp.left_shift(jnp.int32(1), 14 - it)
            return jnp.where(count_half_ge(cand) >= k_wanted, cand, prefix)
        return lax.fori_loop(0, 15, step, prefix0)

    hi = kth_largest_half(kf)
    above = jnp.where(hi == I16_MAX, 0.0, count_half_ge(jnp.minimum(hi + 1, I16_MAX)))

    def low_halves(j, carry):
        key = key_ref[rows_of(j, rb), :]
        low = (key & 0xFFFF) + I16_MIN
        half_ref[rows_of(j, rb), :] = jnp.where((key >> 16) == hi, low, I16_MIN).astype(jnp.int16)
        return carry

    lax.fori_loop(0, n_scan, low_halves, 0)
    lo = kth_largest_half(kf - above)
    thr = (hi << 16) | (lo - I16_MIN)
    beyond = jnp.where(lo == I16_MAX, 0.0, count_half_ge(jnp.minimum(lo + 1, I16_MAX)))
    need = kf - above - beyond
    n_eq = count_half_ge(lo) - beyond

    idx_bits = max(1, (key_ref.shape[0] - 1).bit_length())

    def tie_limit():
        def fill(j, carry):
            tied = key_ref[rows_of(j, rb), :] == thr
            half_ref[rows_of(j, rb), :] = jnp.where(tied, scan_row + j * rb, I16_MAX).astype(jnp.int16)
            return carry

        lax.fori_loop(0, n_scan, fill, 0)
        lanes = lax.broadcasted_iota(jnp.int32, (1, tq), 1)
        scanned = jnp.where(lanes < dead, (n_scan - 1) * rb, n_scan * rb).astype(F32)

        def step(it, prefix):
            cand = prefix + jnp.left_shift(jnp.int32(1), idx_bits - 1 - it)
            below = scanned - count_half_ge(cand)
            return jnp.where(below < need, cand, prefix)
        return lax.fori_loop(0, idx_bits, step, jnp.zeros((1, tq), jnp.int32))

    overfull = jnp.where((n_eq > need) & (thr > INT_MIN), 1.0, 0.0)
    has_ties = jnp.max(overfull) > 0.0

    @pl.when(has_ties)
    def _():
        limit = tie_limit()

        def demote(j, carry):
            key = key_ref[rows_of(j, rb), :]
            drop = (key == thr) & (scan_row + j * rb > limit)
            key_ref[rows_of(j, rb), :] = jnp.where(drop, jnp.int32(INT_MIN), key)
            return carry

        lax.fori_loop(0, n_scan, demote, 0)

    floor = jnp.maximum(thr, jnp.int32(INT_MIN + 1))

    def attend():
        acc_ref[...] = jnp.zeros(acc_ref.shape, F32)
        ones_rows = jnp.ones((BF16_TILE_ROWS, kb), BF16)

        def block(j, carry):
            ms, ls = carry
            bias = jnp.where(key_ref[rows_of(j), :] >= floor, 0.0, MASKED)
            heads = [slice(h * HEAD_DIM, (h + 1) * HEAD_DIM) for h in range(ATT_HEADS)]
            logits = [lax.dot_general(ak_ref[rows_of(j), sl], aq_ref[:, sl], nt,
                                      preferred_element_type=F32) for sl in heads]
            new_ms, probs, rescales = [], [], []
            for h in range(ATT_HEADS):
                s = logits[h] + bias
                m_new = jnp.maximum(ms[h], jnp.max(s, axis=0, keepdims=True))
                probs.append(jnp.exp2(s - m_new).astype(BF16))
                rescales.append(jnp.exp2(ms[h] - m_new))
                new_ms.append(m_new)
            new_ls = []
            for h, sl in enumerate(heads):
                v_aug = jnp.concatenate([avT_ref[0, j, sl, :], ones_rows], axis=0)
                pv = jnp.dot(v_aug, probs[h], preferred_element_type=F32)
                acc_ref[h] = rescales[h] * acc_ref[h] + pv[:HEAD_DIM]
                new_ls.append(rescales[h] * ls[h] + pv[HEAD_DIM:HEAD_DIM + 1])
            return tuple(new_ms), tuple(new_ls)

        init = ((jnp.full((1, tq), MASKED, F32),) * ATT_HEADS,
                (jnp.zeros((1, tq), F32),) * ATT_HEADS)
        _, ls = lax.fori_loop(0, n_blk, block, init)
        for h in range(ATT_HEADS):
            sl = slice(h * HEAD_DIM, (h + 1) * HEAD_DIM)
            o_ref[:, sl] = (acc_ref[h] / ls[h]).T.astype(o_ref.dtype)

    attend()


def _indexed_attention(iq, ik, iwT, aq, ak, avT, batch, seq):
    m = iq.shape[0]
    tq = _tiles(seq)["attn_q"]
    kb = avT.shape[3]
    nq = seq // tq
    topk = min(TOPK_MAX, seq // 4)
    q_spec = pl.BlockSpec((tq, GROUP_WIDTH), lambda b, i: (b * nq + i, 0))
    in_specs = [
        q_spec,
        pl.BlockSpec((seq, V7X_LANES), lambda b, i: (b, 0)),
        pl.BlockSpec((1, IDX_HEADS, tq), lambda b, i: (b, 0, i)),
        q_spec,
        pl.BlockSpec((seq, GROUP_WIDTH), lambda b, i: (b, 0)),
        pl.BlockSpec((1, seq // kb, GROUP_WIDTH, kb), lambda b, i: (b, 0, 0, 0)),
    ]
    scratch = [
        pltpu.VMEM((seq, tq), jnp.int32),
        pltpu.VMEM((seq, tq), jnp.int16),
        pltpu.VMEM((ATT_HEADS, HEAD_DIM, tq), F32),
    ]
    return pl.pallas_call(
        functools.partial(_attn_kernel, topk, _tiles(seq)["attn_scan_rows"]), grid=(batch, nq),
        in_specs=in_specs, out_specs=q_spec,
        out_shape=jax.ShapeDtypeStruct((m, GROUP_WIDTH), BF16),
        scratch_shapes=scratch,
        compiler_params=_params(2), name="indexer_topk_attention",
    )(iq, ik, iwT, aq, ak, avT)


def _layer_norm(y, gain, bias):
    mu = jnp.mean(y, axis=-1, keepdims=True)
    cen = y - mu
    var = jnp.mean(cen * cen, axis=-1, keepdims=True)
    return cen * lax.rsqrt(var + LN_EPS) * gain + bias


def _ffn_chunks(hidden):
    mxu_cols = 256
    assert hidden % mxu_cols == 0
    half = (hidden // mxu_cols + 1) // 2 * mxu_cols
    return [(0, half), (half, hidden)] if half < hidden else [(0, hidden)]


def _tail_kernel(alpha, ret_ref, att_ref, x_ref, wo_ref, g1_ref, b1_ref,
                 wgu_ref, wd_ref, g2_ref, b2_ref, o_ref):
    width = ret_ref.shape[1]
    hidden = wd_ref.shape[0]
    part = ret_ref.shape[0] // TAIL_ROW_PARTS
    halves = [slice(i * part, (i + 1) * part) for i in range(TAIL_ROW_PARTS)]

    def mixed(rows):
        mix = jnp.dot(ret_ref[rows, :], wo_ref[0:width, :], preferred_element_type=F32)
        mix = mix + jnp.dot(att_ref[rows, :], wo_ref[width:, :], preferred_element_type=F32)
        return _layer_norm(alpha * x_ref[rows, :] + mix, g1_ref[...], b1_ref[...])

    def ffn_chunk(xb, lo, hi):
        gate = jnp.dot(xb, wgu_ref[:, lo:hi], preferred_element_type=F32)
        up = jnp.dot(xb, wgu_ref[:, hidden + lo:hidden + hi], preferred_element_type=F32)
        act = (gate * jax.nn.sigmoid(gate) * up).astype(BF16)
        return jnp.dot(act, wd_ref[lo:hi, :], preferred_element_type=F32)

    xs = [mixed(rows) for rows in halves]
    xbs = [x.astype(BF16) for x in xs]
    accs = [alpha * x for x in xs]
    for lo, hi in _ffn_chunks(hidden):
        accs = [acc + ffn_chunk(xb, lo, hi) for acc, xb in zip(accs, xbs)]
    for rows, acc in zip(halves, accs):
        o_ref[rows, :] = _layer_norm(acc, g2_ref[...], b2_ref[...])


def _dense_tail(ret, att, x2d, w_out, g1, b1, w_gate_up, w_down, g2, b2, alpha, seq):
    m, d = x2d.shape
    tm = _tiles(seq)["dense_rows"]
    rows = lambda width: pl.BlockSpec((tm, width), lambda i: (i, 0))
    consts = [w_out, g1, b1, w_gate_up, w_down, g2, b2]
    return pl.pallas_call(
        functools.partial(_tail_kernel, alpha), grid=(m // tm,),
        in_specs=[rows(GROUP_WIDTH), rows(GROUP_WIDTH), rows(d)] + [_resident(c.shape) for c in consts],
        out_specs=rows(d), out_shape=jax.ShapeDtypeStruct((m, d), F32),
        compiler_params=_params(1), name="outproj_ffn_deepnorm",
    )(ret, att, x2d, *consts)


def _rope_frequencies():
    def inv_freq(dim):
        return ROPE_THETA ** (-jnp.arange(0, dim, 2, dtype=F32) / dim)
    return jnp.concatenate([inv_freq(HEAD_DIM), inv_freq(IDX_DIM), inv_freq(IDX_DIM)])[None, :]


def _split_w_in(w_in_l):
    g = GROUP_WIDTH
    w_main = w_in_l[:, :8 * g].astype(BF16)
    w_avT = w_in_l[:, 6 * g:7 * g].T.astype(BF16)
    ik = w_in_l[:, 8 * g:8 * g + IDX_DIM]
    iw = w_in_l[:, 8 * g + IDX_DIM:8 * g + IDX_DIM + IDX_HEADS]
    pad = jnp.zeros((w_in_l.shape[0], V7X_LANES - IDX_HEADS), w_in_l.dtype)
    w_last = jnp.concatenate([ik, ik, iw, pad], axis=1).astype(BF16)
    return w_main, w_avT, w_last


def kernel(x, positions, w_in, ret_gn_gain, w_out, ln_mix_gain, ln_mix_bias,
           w_gate_up, w_down, ln_ffn_gain, ln_ffn_bias):
    batch, seq, d = x.shape
    depth = w_in.shape[0]
    alpha = (2.0 * depth) ** 0.25
    freqs = _rope_frequencies()
    pos2d = positions.reshape(batch * seq, 1)
    x2d = x.reshape(batch * seq, d)
    for l in range(depth):
        w_main, w_avT, w_last = _split_w_in(w_in[l])
        rq, rk, rv, rg, aq, ak, avT, iq, ik, iwT = _inproj(
            x2d, pos2d, freqs, w_main, w_avT, w_last, batch, seq)
        ret = _retention(rq, rk, rv, rg, ret_gn_gain[l][None, :], batch, seq)
        att = _indexed_attention(iq, ik, iwT, aq, ak, avT, batch, seq)
        x2d = _dense_tail(ret, att, x2d, w_out[l].astype(BF16), ln_mix_gain[l][None, :],
                          ln_mix_bias[l][None, :], w_gate_up[l].astype(BF16), w_down[l].astype(BF16),
                          ln_ffn_gain[l][None, :], ln_ffn_bias[l][None, :], alpha, seq)
    return x2d.reshape(batch, seq, d)
```

```python
import functools
import math

import jax
import jax.numpy as jnp
from jax import lax
from jax.experimental import pallas as pl
from jax.experimental.pallas import tpu as pltpu

RET_HEADS = 4
ATT_HEADS = 4
HEAD_DIM = 128
IDX_HEADS = 8
IDX_DIM = 64
GROUP_WIDTH = 512
TOPK_MAX = 256
ROPE_THETA = 10000.0
RET_CHUNK = 128
RET_STEP_CHUNKS = 4
TAIL_ROW_PARTS = 2
LN_EPS = 1e-5

V7X_LANES = 128
BF16_TILE_ROWS = 16
I16_TILE_ROWS = 16
V7X_VMEM_BYTES = 64 * 1024 * 1024
VMEM_LIMIT_BYTES = V7X_VMEM_BYTES - 8 * 1024 * 1024

F32 = jnp.float32
BF16 = jnp.bfloat16
INT_MIN = -(2 ** 31)
I16_BITS = 16
I16_MIN, I16_MAX = -(2 ** (I16_BITS - 1)), 2 ** (I16_BITS - 1) - 1
MASKED = -1e30


def _tiles(seq):
    t = {"proj_rows": min(512, seq), "attn_q": min(512, seq), "attn_keys": min(512, seq),
         "attn_scan_rows": min(256, seq), "dense_rows": min(512, seq)}
    for v in t.values():
        assert seq % v == 0
    assert t["proj_rows"] % t["attn_keys"] == 0 and t["attn_q"] % t["attn_keys"] == 0
    assert t["attn_keys"] % t["attn_scan_rows"] == 0
    return t


def _params(n_axes):
    return pltpu.CompilerParams(dimension_semantics=("arbitrary",) * n_axes,
                                vmem_limit_bytes=VMEM_LIMIT_BYTES)


def _resident(shape):
    zeros = (0,) * len(shape)
    return pl.BlockSpec(shape, lambda *_: zeros, pipeline_mode=pl.Buffered(1))


def _inproj_kernel(x_ref, pos_ref, freq_ref, wm_ref, wavT_ref, wl_ref,
                   rq_ref, rk_ref, rv_ref, rg_ref, aq_ref, ak_ref, avT_ref,
                   iq_ref, ik_ref, iwT_ref):
    xb = x_ref[...].astype(BF16)
    pos = pos_ref[...].astype(F32)
    lane = lax.broadcasted_iota(jnp.int32, (xb.shape[0], V7X_LANES), 1)

    ang = pos * freq_ref[...]
    cos, sin = jnp.cos(ang), jnp.sin(ang)
    low_lanes = lane < V7X_LANES // 2
    cos_swapped = pltpu.roll(cos, V7X_LANES // 2, 1)
    sin_swapped = pltpu.roll(sin, V7X_LANES // 2, 1)
    cos128 = jnp.where(low_lanes, cos, cos_swapped)
    sin128 = jnp.where(low_lanes, -sin, sin_swapped)
    first64 = (lane % IDX_DIM) < IDX_DIM // 2
    cos64 = jnp.where(low_lanes, cos_swapped, cos)
    sin64 = jnp.where(low_lanes, sin_swapped, sin)
    sin64 = jnp.where(first64, -sin64, sin64)

    def proj(group):
        c0 = group * GROUP_WIDTH
        return jnp.dot(xb, wm_ref[:, c0:c0 + GROUP_WIDTH], preferred_element_type=F32)

    def rope128(y, scale):
        c, s = cos128 * scale, sin128 * scale
        return y * c + pltpu.roll(y, HEAD_DIM // 2, 1) * s

    def rope64(y, scale):
        c, s = cos64 * scale, sin64 * scale
        partner = jnp.where(first64, pltpu.roll(y, V7X_LANES - IDX_DIM // 2, 1),
                            pltpu.roll(y, IDX_DIM // 2, 1))
        return y * c + partner * s

    def store_roped(out_ref, y, fn, scale):
        for h in range(GROUP_WIDTH // V7X_LANES):
            sl = slice(h * V7X_LANES, (h + 1) * V7X_LANES)
            out_ref[:, sl] = fn(y[:, sl], scale).astype(out_ref.dtype)

    qk_scale = HEAD_DIM ** -0.5
    store_roped(rq_ref, proj(0), rope128, 1.0)
    store_roped(rk_ref, proj(1), rope128, qk_scale)
    rv_ref[...] = proj(2).astype(BF16)
    rg_ref[...] = proj(3).astype(BF16)
    store_roped(aq_ref, proj(4), rope128, qk_scale * math.log2(math.e))
    store_roped(ak_ref, proj(5), rope128, 1.0)
    store_roped(iq_ref, proj(7), rope64, IDX_DIM ** -0.5)

    avT = lax.dot_general(wavT_ref[...], xb, (((1,), (1,)), ((), ())),
                          preferred_element_type=F32)
    kb = avT_ref.shape[3]
    for j in range(avT_ref.shape[1]):
        avT_ref[0, j] = avT[:, j * kb:(j + 1) * kb].astype(BF16)

    yl = jnp.dot(xb, wl_ref[...], preferred_element_type=F32)
    ik_ref[...] = rope64(yl[:, :V7X_LANES], 1.0).astype(BF16)
    iwT = yl[:, V7X_LANES:].T
    iwT_ref[0] = iwT[:IDX_HEADS, :] * (IDX_HEADS ** -0.5)


def _inproj(x2d, pos2d, freqs, w_main, w_avT, w_last, batch, seq):
    m, d = x2d.shape
    tm = _tiles(seq)["proj_rows"]
    kb = _tiles(seq)["attn_keys"]
    nj = seq // tm
    row = lambda b, j: (b * nj + j, 0)
    rows_spec = lambda width: pl.BlockSpec((tm, width), row)
    out_shape = [jax.ShapeDtypeStruct((m, GROUP_WIDTH), BF16)] * 6 + [
        jax.ShapeDtypeStruct((batch, seq // kb, GROUP_WIDTH, kb), BF16),
        jax.ShapeDtypeStruct((m, GROUP_WIDTH), BF16),
        jax.ShapeDtypeStruct((m, V7X_LANES), BF16),
        jax.ShapeDtypeStruct((batch, IDX_HEADS, seq), F32),
    ]
    out_specs = [rows_spec(GROUP_WIDTH)] * 6 + [
        pl.BlockSpec((1, tm // kb, GROUP_WIDTH, kb), lambda b, j: (b, j, 0, 0)),
        rows_spec(GROUP_WIDTH),
        rows_spec(V7X_LANES),
        pl.BlockSpec((1, IDX_HEADS, tm), lambda b, j: (b, 0, j)),
    ]
    in_specs = [rows_spec(d), rows_spec(1), _resident(freqs.shape),
                _resident(w_main.shape), _resident(w_avT.shape), _resident(w_last.shape)]
    return pl.pallas_call(
        _inproj_kernel, grid=(batch, nj), in_specs=in_specs, out_specs=out_specs,
        out_shape=out_shape, compiler_params=_params(2), name="inproj_rope",
    )(x2d, pos2d, freqs, w_main, w_avT, w_last)


def _retention_kernel(q_ref, k_ref, v_ref, g_ref, gain_ref, o_ref):
    seq = q_ref.shape[0]
    c = RET_CHUNK
    n_chunks = seq // c
    ri = lax.broadcasted_iota(jnp.int32, (c, c), 0).astype(F32)
    ci = lax.broadcasted_iota(jnp.int32, (c, c), 1).astype(F32)
    rel = ri - ci
    pos_col = lax.broadcasted_iota(jnp.int32, (c, 1), 0).astype(F32)

    log_g = [math.log(1.0 - 2.0 ** (-5.0 - h)) for h in range(RET_HEADS)]
    decay_intra = [jnp.where(rel >= 0, jnp.exp(lg * jnp.maximum(rel, 0.0)), 0.0) for lg in log_g]
    decay_q = [jnp.exp(lg * (pos_col + 1.0)) for lg in log_g]
    decay_k = [jnp.exp(lg * (c - 1.0 - pos_col)) for lg in log_g]

    nt = (((1,), (1,)), ((), ()))
    heads = [slice(h * HEAD_DIM, (h + 1) * HEAD_DIM) for h in range(RET_HEADS)]
    chunk_decay = [math.exp(lg * c) for lg in log_g]

    def step(n, states):
        rows = [pl.ds(pl.multiple_of((n * RET_STEP_CHUNKS + i) * c, c), c)
                for i in range(RET_STEP_CHUNKS)]
        q = [[q_ref[r, sl] for r in rows] for sl in heads]
        k = [[k_ref[r, sl] for r in rows] for sl in heads]
        v = [[v_ref[r, sl] for r in rows] for sl in heads]
        scores = [[lax.dot_general(q[h][i], k[h][i], nt, preferred_element_type=F32)
                   for i in range(RET_STEP_CHUNKS)] for h in range(RET_HEADS)]
        kv = [[jnp.dot((k[h][i].astype(F32) * decay_k[h]).T.astype(BF16), v[h][i],
                       preferred_element_type=F32)
               for i in range(RET_STEP_CHUNKS)] for h in range(RET_HEADS)]
        seen = []
        for h in range(RET_HEADS):
            chain = [states[h]]
            for i in range(RET_STEP_CHUNKS):
                chain.append(chunk_decay[h] * chain[-1] + kv[h][i])
            seen.append(chain)
        weights = [[(scores[h][i] * decay_intra[h]).astype(BF16)
                    for i in range(RET_STEP_CHUNKS)] for h in range(RET_HEADS)]
        outs = [[jnp.dot(weights[h][i], v[h][i], preferred_element_type=F32)
                 + jnp.dot(q[h][i], seen[h][i].astype(BF16), preferred_element_type=F32) * decay_q[h]
                 for i in range(RET_STEP_CHUNKS)] for h in range(RET_HEADS)]
        for h, sl in enumerate(heads):
            for i, r in enumerate(rows):
                out = outs[h][i]
                mu = jnp.mean(out, axis=-1, keepdims=True)
                cen = out - mu
                var = jnp.mean(cen * cen, axis=-1, keepdims=True)
                y = cen * lax.rsqrt(var + LN_EPS) * gain_ref[:, sl]
                g = g_ref[r, sl].astype(F32)
                o_ref[r, sl] = (g * jax.nn.sigmoid(g) * y).astype(o_ref.dtype)
        return tuple(chain[-1] for chain in seen)

    assert n_chunks % RET_STEP_CHUNKS == 0
    zero = jnp.zeros((HEAD_DIM, HEAD_DIM), F32)
    lax.fori_loop(0, n_chunks // RET_STEP_CHUNKS, step, (zero,) * RET_HEADS)


def _retention(rq, rk, rv, rg, gain, batch, seq):
    m = rq.shape[0]
    blk = pl.BlockSpec((seq, GROUP_WIDTH), lambda b: (b, 0))
    return pl.pallas_call(
        _retention_kernel, grid=(batch,),
        in_specs=[blk, blk, blk, blk, _resident(gain.shape)], out_specs=blk,
        out_shape=jax.ShapeDtypeStruct((m, GROUP_WIDTH), BF16),
        compiler_params=_params(1), name="retention_gn_gate",
    )(rq, rk, rv, rg, gain)


def _sortable_key(score):
    score = jnp.where(score == 0.0, 0.0, score)
    bits = lax.bitcast_convert_type(score, jnp.int32)
    return jnp.where(bits < 0, bits ^ jnp.int32(0x7FFFFFFF), bits)


def _attn_kernel(topk, rb, iq_ref, ik_ref, iwT_ref, aq_ref, ak_ref, avT_ref, o_ref,
                 key_ref, half_ref, acc_ref):
    tq = iq_ref.shape[0]
    kb = avT_ref.shape[3]
    qi = pl.program_id(1)
    n_blk = ((qi + 1) * tq) // kb
    n_scan = n_blk * (kb // rb)
    kf = float(topk)
    nt = (((1,), (1,)), ((), ()))

    def rows_of(j, size=kb):
        return pl.ds(pl.multiple_of(j * size, size), size)

    def index_tile(j, r0, n_rows, l0, on_diagonal):
        rows = pl.ds(pl.multiple_of(j * kb + r0, rb), n_rows)
        kk = ik_ref[rows, :]
        lane = lax.broadcasted_iota(jnp.int32, kk.shape, 1)
        k_lo = jnp.where(lane < IDX_DIM, kk, jnp.zeros_like(kk))
        k_hi = jnp.where(lane >= IDX_DIM, kk, jnp.zeros_like(kk))
        logits = []
        for pair in range(IDX_HEADS // 2):
            qp = iq_ref[l0:, pair * V7X_LANES:(pair + 1) * V7X_LANES]
            logits += [lax.dot_general(kmat, qp, nt, preferred_element_type=F32)
                       for kmat in (k_lo, k_hi)]
        score = jnp.zeros((n_rows, tq - l0), F32)
        for h in range(IDX_HEADS):
            score = score + jnp.maximum(logits[h], 0.0) * iwT_ref[0, h:h + 1, l0:]
        key = _sortable_key(score)
        if on_diagonal:
            key_pos = lax.broadcasted_iota(jnp.int32, key.shape, 0) + (j * kb + r0)
            query_pos = lax.broadcasted_iota(jnp.int32, key.shape, 1) + (qi * tq + l0)
            key = jnp.where(key_pos <= query_pos, key, jnp.int32(INT_MIN))
        key_ref[rows, l0:] = key
        half_ref[rows, l0:] = (key >> I16_BITS).astype(jnp.int16)
        if l0:
            key_ref[rows, :l0] = jnp.full((n_rows, l0), INT_MIN, jnp.int32)
            half_ref[rows, :l0] = jnp.full((n_rows, l0), I16_MIN, jnp.int16)

    def index_below(j, carry):
        index_tile(j, 0, kb, 0, on_diagonal=False)
        return carry

    assert tq == kb
    lax.fori_loop(0, n_blk - 1, index_below, 0)
    for r in range(0, kb, rb):
        index_tile(n_blk - 1, r, rb, r, on_diagonal=True)

    dead = tq - rb

    def count_half_ge(cand):
        cand16 = cand.astype(jnp.int16)

        def partial_count(values, cand_part):
            ones = jnp.where(values >= cand_part, jnp.int16(1), jnp.int16(0))
            parts = [ones[r:r + I16_TILE_ROWS, :] for r in range(0, rb, I16_TILE_ROWS)]
            while len(parts) > 1:
                parts = [a + b for a, b in zip(parts[0::2], parts[1::2])]
            return parts[0].astype(jnp.int32)

        def body(j, acc):
            return acc + partial_count(half_ref[rows_of(j, rb), :], cand16)
        acc = lax.fori_loop(0, n_scan - 1, body, jnp.zeros((I16_TILE_ROWS, tq), jnp.int32))
        last = partial_count(half_ref[rows_of(n_scan - 1, rb), dead:], cand16[:, dead:])
        acc = jnp.concatenate([acc[:, :dead], acc[:, dead:] + last], axis=1)
        return jnp.sum(acc.astype(F32), axis=0, keepdims=True)

    def kth_largest_half(k_wanted):
        prefix0 = jnp.where(count_half_ge(jnp.zeros((1, tq), jnp.int32)) >= k_wanted,
                            jnp.int32(0), jnp.int32(I16_MIN))

        def step(it, prefix):
            cand = prefix | jnp.left_shift(jnp.int32(1), I16_BITS - 2 - it)
            return jnp.where(count_half_ge(cand) >= k_wanted, cand, prefix)
        return lax.fori_loop(0, I16_BITS - 1, step, prefix0)

    hi = kth_largest_half(kf)
    above = jnp.where(hi == I16_MAX, 0.0, count_half_ge(jnp.minimum(hi + 1, I16_MAX)))

    def low_halves(j, carry):
        key = key_ref[rows_of(j, rb), :]
        low = (key & ((1 << I16_BITS) - 1)) + I16_MIN
        half_ref[rows_of(j, rb), :] = jnp.where((key >> I16_BITS) == hi, low, I16_MIN).astype(jnp.int16)
        return carry

    lax.fori_loop(0, n_scan, low_halves, 0)
    lo = kth_largest_half(kf - above)
    thr = (hi << I16_BITS) | (lo - I16_MIN)
    beyond = jnp.where(lo == I16_MAX, 0.0, count_half_ge(jnp.minimum(lo + 1, I16_MAX)))
    need = kf - above - beyond
    n_eq = count_half_ge(lo) - beyond

    overfull = jnp.where((n_eq > need) & (thr > INT_MIN), 1.0, 0.0)
    has_ties = jnp.max(overfull) > 0.0

    @pl.when(has_ties)
    def _():
        r = lax.broadcasted_iota(jnp.int32, (kb, kb), 0)
        c = lax.broadcasted_iota(jnp.int32, (kb, kb), 1)
        lower = jnp.where(r >= c, 1.0, 0.0).astype(BF16)

        def demote(j, seen):
            key = key_ref[rows_of(j), :]
            tied = key == thr
            rank = seen + jnp.dot(lower, jnp.where(tied, 1.0, 0.0).astype(BF16),
                                  preferred_element_type=F32)
            key_ref[rows_of(j), :] = jnp.where(tied & (rank > need), jnp.int32(INT_MIN), key)
            return rank[kb - 1:kb, :]

        lax.fori_loop(0, n_blk, demote, jnp.zeros((1, tq), F32))

    floor = jnp.maximum(thr, jnp.int32(INT_MIN + 1))

    def attend():
        acc_ref[...] = jnp.zeros(acc_ref.shape, F32)
        ones_rows = jnp.ones((BF16_TILE_ROWS, kb), BF16)

        def block(j, carry):
            ms, ls = carry
            bias = jnp.where(key_ref[rows_of(j), :] >= floor, 0.0, MASKED)
            heads = [slice(h * HEAD_DIM, (h + 1) * HEAD_DIM) for h in range(ATT_HEADS)]
            logits = [lax.dot_general(ak_ref[rows_of(j), sl], aq_ref[:, sl], nt,
                                      preferred_element_type=F32) for sl in heads]
            new_ms, probs, rescales = [], [], []
            for h in range(ATT_HEADS):
                s = logits[h] + bias
                m_new = jnp.maximum(ms[h], jnp.max(s, axis=0, keepdims=True))
                probs.append(jnp.exp2(s - m_new).astype(BF16))
                rescales.append(jnp.exp2(ms[h] - m_new))
                new_ms.append(m_new)
            new_ls = []
            for h, sl in enumerate(heads):
                v_aug = jnp.concatenate([avT_ref[0, j, sl, :], ones_rows], axis=0)
                pv = jnp.dot(v_aug, probs[h], preferred_element_type=F32)
                acc_ref[h] = rescales[h] * acc_ref[h] + pv[:HEAD_DIM]
                new_ls.append(rescales[h] * ls[h] + pv[HEAD_DIM:HEAD_DIM + 1])
            return tuple(new_ms), tuple(new_ls)

        init = ((jnp.full((1, tq), MASKED, F32),) * ATT_HEADS,
                (jnp.zeros((1, tq), F32),) * ATT_HEADS)
        _, ls = lax.fori_loop(0, n_blk, block, init)
        for h in range(ATT_HEADS):
            sl = slice(h * HEAD_DIM, (h + 1) * HEAD_DIM)
            o_ref[:, sl] = (acc_ref[h] / ls[h]).T.astype(o_ref.dtype)

    attend()


def _indexed_attention(iq, ik, iwT, aq, ak, avT, batch, seq):
    m = iq.shape[0]
    tq = _tiles(seq)["attn_q"]
    kb = avT.shape[3]
    nq = seq // tq
    topk = min(TOPK_MAX, seq // 4)
    q_spec = pl.BlockSpec((tq, GROUP_WIDTH), lambda b, i: (b * nq + i, 0))
    in_specs = [
        q_spec,
        pl.BlockSpec((seq, V7X_LANES), lambda b, i: (b, 0)),
        pl.BlockSpec((1, IDX_HEADS, tq), lambda b, i: (b, 0, i)),
        q_spec,
        pl.BlockSpec((seq, GROUP_WIDTH), lambda b, i: (b, 0)),
        pl.BlockSpec((1, seq // kb, GROUP_WIDTH, kb), lambda b, i: (b, 0, 0, 0)),
    ]
    scratch = [
        pltpu.VMEM((seq, tq), jnp.int32),
        pltpu.VMEM((seq, tq), jnp.int16),
        pltpu.VMEM((ATT_HEADS, HEAD_DIM, tq), F32),
    ]
    return pl.pallas_call(
        functools.partial(_attn_kernel, topk, _tiles(seq)["attn_scan_rows"]), grid=(batch, nq),
        in_specs=in_specs, out_specs=q_spec,
        out_shape=jax.ShapeDtypeStruct((m, GROUP_WIDTH), BF16),
        scratch_shapes=scratch,
        compiler_params=_params(2), name="indexer_topk_attention",
    )(iq, ik, iwT, aq, ak, avT)


def _layer_norm(y, gain, bias):
    mu = jnp.mean(y, axis=-1, keepdims=True)
    cen = y - mu
    var = jnp.mean(cen * cen, axis=-1, keepdims=True)
    return cen * lax.rsqrt(var + LN_EPS) * gain + bias


def _ffn_chunks(hidden):
    mxu_cols = 256
    assert hidden % mxu_cols == 0
    half = (hidden // mxu_cols + 1) // 2 * mxu_cols
    return [(0, half), (half, hidden)] if half < hidden else [(0, hidden)]


def _tail_kernel(alpha, ret_ref, att_ref, x_ref, wo_ref, g1_ref, b1_ref,
                 wgu_ref, wd_ref, g2_ref, b2_ref, o_ref):
    width = ret_ref.shape[1]
    hidden = wd_ref.shape[0]
    part = ret_ref.shape[0] // TAIL_ROW_PARTS
    halves = [slice(i * part, (i + 1) * part) for i in range(TAIL_ROW_PARTS)]

    def mixed(rows):
        mix = jnp.dot(ret_ref[rows, :], wo_ref[0:width, :], preferred_element_type=F32)
        mix = mix + jnp.dot(att_ref[rows, :], wo_ref[width:, :], preferred_element_type=F32)
        return _layer_norm(alpha * x_ref[rows, :] + mix, g1_ref[...], b1_ref[...])

    def ffn_chunk(xb, lo, hi):
        gate = jnp.dot(xb, wgu_ref[:, lo:hi], preferred_element_type=F32)
        up = jnp.dot(xb, wgu_ref[:, hidden + lo:hidden + hi], preferred_element_type=F32)
        act = (gate * jax.nn.sigmoid(gate) * up).astype(BF16)
        return jnp.dot(act, wd_ref[lo:hi, :], preferred_element_type=F32)

    xs = [mixed(rows) for rows in halves]
    xbs = [x.astype(BF16) for x in xs]
    accs = [alpha * x for x in xs]
    for lo, hi in _ffn_chunks(hidden):
        accs = [acc + ffn_chunk(xb, lo, hi) for acc, xb in zip(accs, xbs)]
    for rows, acc in zip(halves, accs):
        o_ref[rows, :] = _layer_norm(acc, g2_ref[...], b2_ref[...])


def _dense_tail(ret, att, x2d, w_out, g1, b1, w_gate_up, w_down, g2, b2, alpha, seq):
    m, d = x2d.shape
    tm = _tiles(seq)["dense_rows"]
    rows = lambda width: pl.BlockSpec((tm, width), lambda i: (i, 0))
    consts = [w_out, g1, b1, w_gate_up, w_down, g2, b2]
    return pl.pallas_call(
        functools.partial(_tail_kernel, alpha), grid=(m // tm,),
        in_specs=[rows(GROUP_WIDTH), rows(GROUP_WIDTH), rows(d)] + [_resident(c.shape) for c in consts],
        out_specs=rows(d), out_shape=jax.ShapeDtypeStruct((m, d), F32),
        compiler_params=_params(1), name="outproj_ffn_deepnorm",
    )(ret, att, x2d, *consts)


def _rope_frequencies():
    def inv_freq(dim):
        return ROPE_THETA ** (-jnp.arange(0, dim, 2, dtype=F32) / dim)
    return jnp.concatenate([inv_freq(HEAD_DIM), inv_freq(IDX_DIM), inv_freq(IDX_DIM)])[None, :]


def _split_w_in(w_in_l):
    g = GROUP_WIDTH
    w_main = w_in_l[:, :8 * g].astype(BF16)
    w_avT = w_in_l[:, 6 * g:7 * g].T.astype(BF16)
    ik = w_in_l[:, 8 * g:8 * g + IDX_DIM]
    iw = w_in_l[:, 8 * g + IDX_DIM:8 * g + IDX_DIM + IDX_HEADS]
    pad = jnp.zeros((w_in_l.shape[0], V7X_LANES - IDX_HEADS), w_in_l.dtype)
    w_last = jnp.concatenate([ik, ik, iw, pad], axis=1).astype(BF16)
    return w_main, w_avT, w_last


def kernel(x, positions, w_in, ret_gn_gain, w_out, ln_mix_gain, ln_mix_bias,
           w_gate_up, w_down, ln_ffn_gain, ln_ffn_bias):
    batch, seq, d = x.shape
    depth = w_in.shape[0]
    alpha = (2.0 * depth) ** 0.25
    freqs = _rope_frequencies()
    pos2d = positions.reshape(batch * seq, 1)
    x2d = x.reshape(batch * seq, d)
    for l in range(depth):
        w_main, w_avT, w_last = _split_w_in(w_in[l])
        rq, rk, rv, rg, aq, ak, avT, iq, ik, iwT = _inproj(
            x2d, pos2d, freqs, w_main, w_avT, w_last, batch, seq)
        ret = _retention(rq, rk, rv, rg, ret_gn_gain[l][None, :], batch, seq)
        att = _indexed_attention(iq, ik, iwT, aq, ak, avT, batch, seq)
        x2d = _dense_tail(ret, att, x2d, w_out[l].astype(BF16), ln_mix_gain[l][None, :],
                          ln_mix_bias[l][None, :], w_gate_up[l].astype(BF16), w_down[l].astype(BF16),
                          ln_ffn_gain[l][None, :], ln_ffn_bias[l][None, :], alpha, seq)
    return x2d.reshape(batch, seq, d)
```

```python
import functools
import math

import jax
import jax.numpy as jnp
from jax import lax
from jax.experimental import pallas as pl
from jax.experimental.pallas import tpu as pltpu

RET_HEADS = 4
ATT_HEADS = 4
HEAD_DIM = 128
IDX_HEADS = 8
IDX_DIM = 64
GROUP_WIDTH = 512
TOPK_MAX = 256
ROPE_THETA = 10000.0
RET_CHUNK = 128
RET_STEP_CHUNKS = 4
TAIL_ROW_PARTS = 2
LN_EPS = 1e-5

V7X_LANES = 128
BF16_TILE_ROWS = 16
I16_TILE_ROWS = 16
V7X_VMEM_BYTES = 64 * 1024 * 1024
VMEM_LIMIT_BYTES = V7X_VMEM_BYTES - 8 * 1024 * 1024

F32 = jnp.float32
BF16 = jnp.bfloat16
INT_MIN = -(2 ** 31)
I16_BITS = 16
I16_MIN, I16_MAX = -(2 ** (I16_BITS - 1)), 2 ** (I16_BITS - 1) - 1
MASKED = -1e30


def _tiles(seq):
    t = {"proj_rows": min(512, seq), "attn_q": min(512, seq), "attn_keys": min(512, seq),
         "attn_scan_rows": min(256, seq), "dense_rows": min(512, seq)}
    for v in t.values():
        assert seq % v == 0
    assert t["proj_rows"] % t["attn_keys"] == 0 and t["attn_q"] % t["attn_keys"] == 0
    assert t["attn_keys"] % t["attn_scan_rows"] == 0
    return t


def _params(n_axes):
    return pltpu.CompilerParams(dimension_semantics=("arbitrary",) * n_axes,
                                vmem_limit_bytes=VMEM_LIMIT_BYTES)


def _resident(shape):
    zeros = (0,) * len(shape)
    return pl.BlockSpec(shape, lambda *_: zeros, pipeline_mode=pl.Buffered(1))


def _layer_resident(stacked_shape, layer, width=None):
    block = (None,) + tuple(stacked_shape[1:-1]) + (width or stacked_shape[-1],)
    index = (layer,) + (0,) * (len(stacked_shape) - 1)
    return pl.BlockSpec(block, lambda *_: index, pipeline_mode=pl.Buffered(1))


def _inproj_kernel(x_ref, pos_ref, freq_ref, wm_ref, wavT_ref, wl_ref,
                   rq_ref, rk_ref, rv_ref, rg_ref, aq_ref, ak_ref, avT_ref,
                   iq_ref, ik_ref, iwT_ref):
    xb = x_ref[...].astype(BF16)
    pos = pos_ref[...].astype(F32)
    lane = lax.broadcasted_iota(jnp.int32, (xb.shape[0], V7X_LANES), 1)

    ang = pos * freq_ref[...]
    cos, sin = jnp.cos(ang), jnp.sin(ang)
    low_lanes = lane < V7X_LANES // 2
    cos_swapped = pltpu.roll(cos, V7X_LANES // 2, 1)
    sin_swapped = pltpu.roll(sin, V7X_LANES // 2, 1)
    cos128 = jnp.where(low_lanes, cos, cos_swapped)
    sin128 = jnp.where(low_lanes, -sin, sin_swapped)
    first64 = (lane % IDX_DIM) < IDX_DIM // 2
    cos64 = jnp.where(low_lanes, cos_swapped, cos)
    sin64 = jnp.where(low_lanes, sin_swapped, sin)
    sin64 = jnp.where(first64, -sin64, sin64)

    def proj(group):
        c0 = group * GROUP_WIDTH
        return jnp.dot(xb, wm_ref[:, c0:c0 + GROUP_WIDTH], preferred_element_type=F32)

    def rope128(y, scale):
        c, s = cos128 * scale, sin128 * scale
        return y * c + pltpu.roll(y, HEAD_DIM // 2, 1) * s

    def rope64(y, scale):
        c, s = cos64 * scale, sin64 * scale
        partner = jnp.where(first64, pltpu.roll(y, V7X_LANES - IDX_DIM // 2, 1),
                            pltpu.roll(y, IDX_DIM // 2, 1))
        return y * c + partner * s

    def store_roped(out_ref, y, fn, scale):
        for h in range(GROUP_WIDTH // V7X_LANES):
            sl = slice(h * V7X_LANES, (h + 1) * V7X_LANES)
            out_ref[:, sl] = fn(y[:, sl], scale).astype(out_ref.dtype)

    qk_scale = HEAD_DIM ** -0.5
    store_roped(rq_ref, proj(0), rope128, 1.0)
    store_roped(rk_ref, proj(1), rope128, qk_scale)
    rv_ref[...] = proj(2).astype(BF16)
    rg_ref[...] = proj(3).astype(BF16)
    store_roped(aq_ref, proj(4), rope128, qk_scale * math.log2(math.e))
    store_roped(ak_ref, proj(5), rope128, 1.0)
    store_roped(iq_ref, proj(7), rope64, IDX_DIM ** -0.5)

    avT = lax.dot_general(wavT_ref[...], xb, (((1,), (1,)), ((), ())),
                          preferred_element_type=F32)
    kb = avT_ref.shape[3]
    for j in range(avT_ref.shape[1]):
        avT_ref[0, j] = avT[:, j * kb:(j + 1) * kb].astype(BF16)

    yl = jnp.dot(xb, wl_ref[...], preferred_element_type=F32)
    ik_ref[...] = rope64(yl[:, :V7X_LANES], 1.0).astype(BF16)
    iwT = yl[:, V7X_LANES:].T
    iwT_ref[0] = iwT[:IDX_HEADS, :] * (IDX_HEADS ** -0.5)


def _inproj(x2d, pos2d, freqs, w_in, layer, w_avT, w_last, batch, seq):
    m, d = x2d.shape
    tm = _tiles(seq)["proj_rows"]
    kb = _tiles(seq)["attn_keys"]
    nj = seq // tm
    row = lambda b, j: (b * nj + j, 0)
    rows_spec = lambda width: pl.BlockSpec((tm, width), row)
    out_shape = [jax.ShapeDtypeStruct((m, GROUP_WIDTH), BF16)] * 6 + [
        jax.ShapeDtypeStruct((batch, seq // kb, GROUP_WIDTH, kb), BF16),
        jax.ShapeDtypeStruct((m, GROUP_WIDTH), BF16),
        jax.ShapeDtypeStruct((m, V7X_LANES), BF16),
        jax.ShapeDtypeStruct((batch, IDX_HEADS, seq), F32),
    ]
    out_specs = [rows_spec(GROUP_WIDTH)] * 6 + [
        pl.BlockSpec((1, tm // kb, GROUP_WIDTH, kb), lambda b, j: (b, j, 0, 0)),
        rows_spec(GROUP_WIDTH),
        rows_spec(V7X_LANES),
        pl.BlockSpec((1, IDX_HEADS, tm), lambda b, j: (b, 0, j)),
    ]
    n_main = (w_in.shape[-1] // GROUP_WIDTH) * GROUP_WIDTH
    in_specs = [rows_spec(d), rows_spec(1), _resident(freqs.shape),
                _layer_resident(w_in.shape, layer, n_main), _resident(w_avT.shape), _resident(w_last.shape)]
    return pl.pallas_call(
        _inproj_kernel, grid=(batch, nj), in_specs=in_specs, out_specs=out_specs,
        out_shape=out_shape, compiler_params=_params(2), name="inproj_rope",
    )(x2d, pos2d, freqs, w_in, w_avT, w_last)


def _retention_kernel(q_ref, k_ref, v_ref, g_ref, gain_ref, o_ref):
    seq = q_ref.shape[0]
    c = RET_CHUNK
    n_chunks = seq // c
    ri = lax.broadcasted_iota(jnp.int32, (c, c), 0).astype(F32)
    ci = lax.broadcasted_iota(jnp.int32, (c, c), 1).astype(F32)
    rel = ri - ci
    pos_col = lax.broadcasted_iota(jnp.int32, (c, 1), 0).astype(F32)

    log_g = [math.log(1.0 - 2.0 ** (-5.0 - h)) for h in range(RET_HEADS)]
    decay_intra = [jnp.where(rel >= 0, jnp.exp(lg * jnp.maximum(rel, 0.0)), 0.0) for lg in log_g]
    decay_q = [jnp.exp(lg * (pos_col + 1.0)) for lg in log_g]
    decay_k = [jnp.exp(lg * (c - 1.0 - pos_col)) for lg in log_g]

    nt = (((1,), (1,)), ((), ()))
    heads = [slice(h * HEAD_DIM, (h + 1) * HEAD_DIM) for h in range(RET_HEADS)]
    chunk_decay = [math.exp(lg * c) for lg in log_g]

    def step(n, states):
        rows = [pl.ds(pl.multiple_of((n * RET_STEP_CHUNKS + i) * c, c), c)
                for i in range(RET_STEP_CHUNKS)]
        q = [[q_ref[r, sl] for r in rows] for sl in heads]
        k = [[k_ref[r, sl] for r in rows] for sl in heads]
        v = [[v_ref[r, sl] for r in rows] for sl in heads]
        scores = [[lax.dot_general(q[h][i], k[h][i], nt, preferred_element_type=F32)
                   for i in range(RET_STEP_CHUNKS)] for h in range(RET_HEADS)]
        kv = [[jnp.dot((k[h][i].astype(F32) * decay_k[h]).T.astype(BF16), v[h][i],
                       preferred_element_type=F32)
               for i in range(RET_STEP_CHUNKS)] for h in range(RET_HEADS)]
        seen = []
        for h in range(RET_HEADS):
            chain = [states[h]]
            for i in range(RET_STEP_CHUNKS):
                chain.append(chunk_decay[h] * chain[-1] + kv[h][i])
            seen.append(chain)
        weights = [[(scores[h][i] * decay_intra[h]).astype(BF16)
                    for i in range(RET_STEP_CHUNKS)] for h in range(RET_HEADS)]
        outs = [[jnp.dot(weights[h][i], v[h][i], preferred_element_type=F32)
                 + jnp.dot(q[h][i], seen[h][i].astype(BF16), preferred_element_type=F32) * decay_q[h]
                 for i in range(RET_STEP_CHUNKS)] for h in range(RET_HEADS)]
        for h, sl in enumerate(heads):
            for i, r in enumerate(rows):
                out = outs[h][i]
                mu = jnp.mean(out, axis=-1, keepdims=True)
                cen = out - mu
                var = jnp.mean(cen * cen, axis=-1, keepdims=True)
                y = cen * lax.rsqrt(var + LN_EPS) * gain_ref[:, sl]
                g = g_ref[r, sl].astype(F32)
                o_ref[r, sl] = (g * jax.nn.sigmoid(g) * y).astype(o_ref.dtype)
        return tuple(chain[-1] for chain in seen)

    assert n_chunks % RET_STEP_CHUNKS == 0
    zero = jnp.zeros((HEAD_DIM, HEAD_DIM), F32)
    lax.fori_loop(0, n_chunks // RET_STEP_CHUNKS, step, (zero,) * RET_HEADS)


def _retention(rq, rk, rv, rg, gain, layer, batch, seq):
    m = rq.shape[0]
    blk = pl.BlockSpec((seq, GROUP_WIDTH), lambda b: (b, 0))
    return pl.pallas_call(
        _retention_kernel, grid=(batch,),
        in_specs=[blk, blk, blk, blk, _layer_resident(gain.shape, layer)], out_specs=blk,
        out_shape=jax.ShapeDtypeStruct((m, GROUP_WIDTH), BF16),
        compiler_params=_params(1), name="retention_gn_gate",
    )(rq, rk, rv, rg, gain)


def _sortable_key(score):
    score = jnp.where(score == 0.0, 0.0, score)
    bits = lax.bitcast_convert_type(score, jnp.int32)
    return jnp.where(bits < 0, bits ^ jnp.int32(0x7FFFFFFF), bits)


def _attn_kernel(topk, rb, iq_ref, ik_ref, iwT_ref, aq_ref, ak_ref, avT_ref, o_ref,
                 key_ref, half_ref, acc_ref):
    tq = iq_ref.shape[0]
    kb = avT_ref.shape[3]
    qi = pl.program_id(1)
    n_blk = ((qi + 1) * tq) // kb
    n_scan = n_blk * (kb // rb)
    kf = float(topk)
    nt = (((1,), (1,)), ((), ()))

    def rows_of(j, size=kb):
        return pl.ds(pl.multiple_of(j * size, size), size)

    def index_tile(j, r0, n_rows, l0, on_diagonal):
        rows = pl.ds(pl.multiple_of(j * kb + r0, rb), n_rows)
        kk = ik_ref[rows, :]
        lane = lax.broadcasted_iota(jnp.int32, kk.shape, 1)
        k_lo = jnp.where(lane < IDX_DIM, kk, jnp.zeros_like(kk))
        k_hi = jnp.where(lane >= IDX_DIM, kk, jnp.zeros_like(kk))
        logits = []
        for pair in range(IDX_HEADS // 2):
            qp = iq_ref[l0:, pair * V7X_LANES:(pair + 1) * V7X_LANES]
            logits += [lax.dot_general(kmat, qp, nt, preferred_element_type=F32)
                       for kmat in (k_lo, k_hi)]
        score = jnp.zeros((n_rows, tq - l0), F32)
        for h in range(IDX_HEADS):
            score = score + jnp.maximum(logits[h], 0.0) * iwT_ref[0, h:h + 1, l0:]
        key = _sortable_key(score)
        if on_diagonal:
            key_pos = lax.broadcasted_iota(jnp.int32, key.shape, 0) + (j * kb + r0)
            query_pos = lax.broadcasted_iota(jnp.int32, key.shape, 1) + (qi * tq + l0)
            key = jnp.where(key_pos <= query_pos, key, jnp.int32(INT_MIN))
        key_ref[rows, l0:] = key
        half_ref[rows, l0:] = (key >> I16_BITS).astype(jnp.int16)
        if l0:
            key_ref[rows, :l0] = jnp.full((n_rows, l0), INT_MIN, jnp.int32)
            half_ref[rows, :l0] = jnp.full((n_rows, l0), I16_MIN, jnp.int16)

    def index_below(j, carry):
        index_tile(j, 0, kb, 0, on_diagonal=False)
        return carry

    assert tq == kb
    lax.fori_loop(0, n_blk - 1, index_below, 0)
    for r in range(0, kb, rb):
        index_tile(n_blk - 1, r, rb, r, on_diagonal=True)

    dead = tq - rb

    def count_half_ge(cand):
        cand16 = cand.astype(jnp.int16)

        def partial_count(values, cand_part):
            ones = jnp.where(values >= cand_part, jnp.int16(1), jnp.int16(0))
            parts = [ones[r:r + I16_TILE_ROWS, :] for r in range(0, rb, I16_TILE_ROWS)]
            while len(parts) > 1:
                parts = [a + b for a, b in zip(parts[0::2], parts[1::2])]
            return parts[0]

        assert key_ref.shape[0] // I16_TILE_ROWS <= I16_MAX

        def body(j, acc):
            return acc + partial_count(half_ref[rows_of(j, rb), :], cand16)
        acc = lax.fori_loop(0, n_scan - 1, body, jnp.zeros((I16_TILE_ROWS, tq), jnp.int16))
        last = partial_count(half_ref[rows_of(n_scan - 1, rb), dead:], cand16[:, dead:])
        acc = jnp.concatenate([acc[:, :dead], acc[:, dead:] + last], axis=1)
        return jnp.sum(acc.astype(jnp.int32).astype(F32), axis=0, keepdims=True)

    def kth_largest_half(k_wanted):
        prefix0 = jnp.where(count_half_ge(jnp.zeros((1, tq), jnp.int32)) >= k_wanted,
                            jnp.int32(0), jnp.int32(I16_MIN))

        def step(it, prefix):
            cand = prefix | jnp.left_shift(jnp.int32(1), I16_BITS - 2 - it)
            return jnp.where(count_half_ge(cand) >= k_wanted, cand, prefix)
        return lax.fori_loop(0, I16_BITS - 1, step, prefix0)

    hi = kth_largest_half(kf)
    above = jnp.where(hi == I16_MAX, 0.0, count_half_ge(jnp.minimum(hi + 1, I16_MAX)))

    def low_halves(j, carry):
        key = key_ref[rows_of(j, rb), :]
        low = (key & ((1 << I16_BITS) - 1)) + I16_MIN
        half_ref[rows_of(j, rb), :] = jnp.where((key >> I16_BITS) == hi, low, I16_MIN).astype(jnp.int16)
        return carry

    lax.fori_loop(0, n_scan, low_halves, 0)
    lo = kth_largest_half(kf - above)
    thr = (hi << I16_BITS) | (lo - I16_MIN)
    beyond = jnp.where(lo == I16_MAX, 0.0, count_half_ge(jnp.minimum(lo + 1, I16_MAX)))
    need = kf - above - beyond
    n_eq = count_half_ge(lo) - beyond

    overfull = jnp.where((n_eq > need) & (thr > INT_MIN), 1.0, 0.0)
    has_ties = jnp.max(overfull) > 0.0

    @pl.when(has_ties)
    def _():
        r = lax.broadcasted_iota(jnp.int32, (kb, kb), 0)
        c = lax.broadcasted_iota(jnp.int32, (kb, kb), 1)
        lower = jnp.where(r >= c, 1.0, 0.0).astype(BF16)

        def demote(j, seen):
            key = key_ref[rows_of(j), :]
            tied = key == thr
            rank = seen + jnp.dot(lower, jnp.where(tied, 1.0, 0.0).astype(BF16),
                                  preferred_element_type=F32)
            key_ref[rows_of(j), :] = jnp.where(tied & (rank > need), jnp.int32(INT_MIN), key)
            return rank[kb - 1:kb, :]

        lax.fori_loop(0, n_blk, demote, jnp.zeros((1, tq), F32))

    floor = jnp.maximum(thr, jnp.int32(INT_MIN + 1))

    def attend():
        acc_ref[...] = jnp.zeros(acc_ref.shape, F32)
        ones_rows = jnp.ones((BF16_TILE_ROWS, kb), BF16)

        def block(j, carry):
            ms, ls = carry
            bias = jnp.where(key_ref[rows_of(j), :] >= floor, 0.0, MASKED)
            heads = [slice(h * HEAD_DIM, (h + 1) * HEAD_DIM) for h in range(ATT_HEADS)]
            logits = [lax.dot_general(ak_ref[rows_of(j), sl], aq_ref[:, sl], nt,
                                      preferred_element_type=F32) for sl in heads]
            new_ms, probs, rescales = [], [], []
            for h in range(ATT_HEADS):
                s = logits[h] + bias
                m_new = jnp.maximum(ms[h], jnp.max(s, axis=0, keepdims=True))
                probs.append(jnp.exp2(s - m_new).astype(BF16))
                rescales.append(jnp.exp2(ms[h] - m_new))
                new_ms.append(m_new)
            new_ls = []
            for h, sl in enumerate(heads):
                v_aug = jnp.concatenate([avT_ref[0, j, sl, :], ones_rows], axis=0)
                pv = jnp.dot(v_aug, probs[h], preferred_element_type=F32)
                acc_ref[h] = rescales[h] * acc_ref[h] + pv[:HEAD_DIM]
                new_ls.append(rescales[h] * ls[h] + pv[HEAD_DIM:HEAD_DIM + 1])
            return tuple(new_ms), tuple(new_ls)

        init = ((jnp.full((1, tq), MASKED, F32),) * ATT_HEADS,
                (jnp.zeros((1, tq), F32),) * ATT_HEADS)
        _, ls = lax.fori_loop(0, n_blk, block, init)
        for h in range(ATT_HEADS):
            sl = slice(h * HEAD_DIM, (h + 1) * HEAD_DIM)
            o_ref[:, sl] = (acc_ref[h] / ls[h]).T.astype(o_ref.dtype)

    attend()


def _indexed_attention(iq, ik, iwT, aq, ak, avT, batch, seq):
    m = iq.shape[0]
    tq = _tiles(seq)["attn_q"]
    kb = avT.shape[3]
    nq = seq // tq
    topk = min(TOPK_MAX, seq // 4)
    q_spec = pl.BlockSpec((tq, GROUP_WIDTH), lambda b, i: (b * nq + i, 0))
    in_specs = [
        q_spec,
        pl.BlockSpec((seq, V7X_LANES), lambda b, i: (b, 0)),
        pl.BlockSpec((1, IDX_HEADS, tq), lambda b, i: (b, 0, i)),
        q_spec,
        pl.BlockSpec((seq, GROUP_WIDTH), lambda b, i: (b, 0)),
        pl.BlockSpec((1, seq // kb, GROUP_WIDTH, kb), lambda b, i: (b, 0, 0, 0)),
    ]
    scratch = [
        pltpu.VMEM((seq, tq), jnp.int32),
        pltpu.VMEM((seq, tq), jnp.int16),
        pltpu.VMEM((ATT_HEADS, HEAD_DIM, tq), F32),
    ]
    return pl.pallas_call(
        functools.partial(_attn_kernel, topk, _tiles(seq)["attn_scan_rows"]), grid=(batch, nq),
        in_specs=in_specs, out_specs=q_spec,
        out_shape=jax.ShapeDtypeStruct((m, GROUP_WIDTH), BF16),
        scratch_shapes=scratch,
        compiler_params=_params(2), name="indexer_topk_attention",
    )(iq, ik, iwT, aq, ak, avT)


def _layer_norm(y, gain, bias):
    mu = jnp.mean(y, axis=-1, keepdims=True)
    cen = y - mu
    var = jnp.mean(cen * cen, axis=-1, keepdims=True)
    return cen * lax.rsqrt(var + LN_EPS) * gain + bias


def _ffn_chunks(hidden):
    mxu_cols = 256
    assert hidden % mxu_cols == 0
    half = (hidden // mxu_cols + 1) // 2 * mxu_cols
    return [(0, half), (half, hidden)] if half < hidden else [(0, hidden)]


def _tail_kernel(alpha, ret_ref, att_ref, x_ref, wo_ref, g1_ref, b1_ref,
                 wgu_ref, wd_ref, g2_ref, b2_ref, o_ref):
    width = ret_ref.shape[1]
    hidden = wd_ref.shape[0]
    part = ret_ref.shape[0] // TAIL_ROW_PARTS
    halves = [slice(i * part, (i + 1) * part) for i in range(TAIL_ROW_PARTS)]

    def mixed(rows):
        mix = jnp.dot(ret_ref[rows, :], wo_ref[0:width, :], preferred_element_type=F32)
        mix = mix + jnp.dot(att_ref[rows, :], wo_ref[width:, :], preferred_element_type=F32)
        return _layer_norm(alpha * x_ref[rows, :] + mix, g1_ref[...], b1_ref[...])

    def ffn_chunk(xb, lo, hi):
        gate = jnp.dot(xb, wgu_ref[:, lo:hi], preferred_element_type=F32)
        up = jnp.dot(xb, wgu_ref[:, hidden + lo:hidden + hi], preferred_element_type=F32)
        act = (gate * jax.nn.sigmoid(gate) * up).astype(BF16)
        return jnp.dot(act, wd_ref[lo:hi, :], preferred_element_type=F32)

    xs = [mixed(rows) for rows in halves]
    xbs = [x.astype(BF16) for x in xs]
    accs = [alpha * x for x in xs]
    for lo, hi in _ffn_chunks(hidden):
        accs = [acc + ffn_chunk(xb, lo, hi) for acc, xb in zip(accs, xbs)]
    for rows, acc in zip(halves, accs):
        o_ref[rows, :] = _layer_norm(acc, g2_ref[...], b2_ref[...])


def _dense_tail(ret, att, x2d, w_out, g1, b1, w_gate_up, w_down, g2, b2, layer, alpha, seq):
    m, d = x2d.shape
    tm = _tiles(seq)["dense_rows"]
    rows = lambda width: pl.BlockSpec((tm, width), lambda i: (i, 0))
    consts = [w_out, g1, b1, w_gate_up, w_down, g2, b2]
    return pl.pallas_call(
        functools.partial(_tail_kernel, alpha), grid=(m // tm,),
        in_specs=[rows(GROUP_WIDTH), rows(GROUP_WIDTH), rows(d)]
        + [_layer_resident(c.shape, layer) for c in consts],
        out_specs=rows(d), out_shape=jax.ShapeDtypeStruct((m, d), F32),
        compiler_params=_params(1), name="outproj_ffn_deepnorm",
    )(ret, att, x2d, *consts)


def _rope_frequencies():
    def inv_freq(dim):
        return ROPE_THETA ** (-jnp.arange(0, dim, 2, dtype=F32) / dim)
    return jnp.concatenate([inv_freq(HEAD_DIM), inv_freq(IDX_DIM), inv_freq(IDX_DIM)])[None, :]


def _small_w_in_parts(w_in_l):
    g = GROUP_WIDTH
    w_avT = w_in_l[:, 6 * g:7 * g].T
    ik = w_in_l[:, 8 * g:8 * g + IDX_DIM]
    iw = w_in_l[:, 8 * g + IDX_DIM:8 * g + IDX_DIM + IDX_HEADS]
    pad = jnp.zeros((w_in_l.shape[0], V7X_LANES - IDX_HEADS), w_in_l.dtype)
    w_last = jnp.concatenate([ik, ik, iw, pad], axis=1)
    return w_avT, w_last


def kernel(x, positions, w_in, ret_gn_gain, w_out, ln_mix_gain, ln_mix_bias,
           w_gate_up, w_down, ln_ffn_gain, ln_ffn_bias):
    batch, seq, d = x.shape
    depth = w_in.shape[0]
    alpha = (2.0 * depth) ** 0.25
    freqs = _rope_frequencies()
    pos2d = positions.reshape(batch * seq, 1)
    x2d = x.reshape(batch * seq, d)
    w_in, w_out, w_gate_up, w_down = (w.astype(BF16) for w in (w_in, w_out, w_gate_up, w_down))
    per_row = lambda p: p[:, None, :]
    for l in range(depth):
        w_avT, w_last = _small_w_in_parts(w_in[l])
        rq, rk, rv, rg, aq, ak, avT, iq, ik, iwT = _inproj(
            x2d, pos2d, freqs, w_in, l, w_avT, w_last, batch, seq)
        ret = _retention(rq, rk, rv, rg, per_row(ret_gn_gain), l, batch, seq)
        att = _indexed_attention(iq, ik, iwT, aq, ak, avT, batch, seq)
        x2d = _dense_tail(ret, att, x2d, w_out, per_row(ln_mix_gain), per_row(ln_mix_bias),
                          w_gate_up, w_down, per_row(ln_ffn_gain), per_row(ln_ffn_bias), l, alpha, seq)
    return x2d.reshape(batch, seq, d)
```

```python
import functools
import math

import jax
import jax.numpy as jnp
from jax import lax
from jax.experimental import pallas as pl
from jax.experimental.pallas import tpu as pltpu

RET_HEADS = 4
ATT_HEADS = 4
HEAD_DIM = 128
IDX_HEADS = 8
IDX_DIM = 64
GROUP_WIDTH = 512
TOPK_MAX = 256
ROPE_THETA = 10000.0
RET_CHUNK = 128
RET_STEP_CHUNKS = 4
TAIL_ROW_PARTS = 2
LN_EPS = 1e-5

V7X_LANES = 128
BF16_TILE_ROWS = 16
I16_TILE_ROWS = 16
V7X_VMEM_BYTES = 64 * 1024 * 1024
VMEM_LIMIT_BYTES = V7X_VMEM_BYTES - 8 * 1024 * 1024

F32 = jnp.float32
BF16 = jnp.bfloat16
INT_MIN = -(2 ** 31)
I16_BITS = 16
I16_MIN, I16_MAX = -(2 ** (I16_BITS - 1)), 2 ** (I16_BITS - 1) - 1
MASKED = -1e30


def _tiles(seq):
    t = {"proj_rows": min(512, seq), "attn_q": min(512, seq), "attn_keys": min(512, seq),
         "attn_scan_rows": min(256, seq), "dense_rows": min(512, seq)}
    for v in t.values():
        assert seq % v == 0
    assert t["proj_rows"] % t["attn_keys"] == 0 and t["attn_q"] % t["attn_keys"] == 0
    assert t["attn_keys"] % t["attn_scan_rows"] == 0
    return t


def _params(n_axes):
    return pltpu.CompilerParams(dimension_semantics=("arbitrary",) * n_axes,
                                vmem_limit_bytes=VMEM_LIMIT_BYTES)


def _resident(shape):
    zeros = (0,) * len(shape)
    return pl.BlockSpec(shape, lambda *_: zeros, pipeline_mode=pl.Buffered(1))


def _layer_resident(stacked_shape, layer, width=None):
    block = (None,) + tuple(stacked_shape[1:-1]) + (width or stacked_shape[-1],)
    index = (layer,) + (0,) * (len(stacked_shape) - 1)
    return pl.BlockSpec(block, lambda *_: index, pipeline_mode=pl.Buffered(1))


def _inproj_kernel(x_ref, pos_ref, freq_ref, wm_ref, wavT_ref, wl_ref,
                   rq_ref, rk_ref, rv_ref, rg_ref, aq_ref, ak_ref, avT_ref,
                   iq_ref, ik_ref, iwT_ref):
    xb = x_ref[...].astype(BF16)
    pos = pos_ref[...].astype(F32)
    lane = lax.broadcasted_iota(jnp.int32, (xb.shape[0], V7X_LANES), 1)

    ang = pos * freq_ref[...]
    cos, sin = jnp.cos(ang), jnp.sin(ang)
    low_lanes = lane < V7X_LANES // 2
    cos_swapped = pltpu.roll(cos, V7X_LANES // 2, 1)
    sin_swapped = pltpu.roll(sin, V7X_LANES // 2, 1)
    cos128 = jnp.where(low_lanes, cos, cos_swapped)
    sin128 = jnp.where(low_lanes, -sin, sin_swapped)
    first64 = (lane % IDX_DIM) < IDX_DIM // 2
    cos64 = jnp.where(low_lanes, cos_swapped, cos)
    sin64 = jnp.where(low_lanes, sin_swapped, sin)
    sin64 = jnp.where(first64, -sin64, sin64)

    def proj(group):
        c0 = group * GROUP_WIDTH
        return jnp.dot(xb, wm_ref[:, c0:c0 + GROUP_WIDTH], preferred_element_type=F32)

    def rope128(y, scale):
        c, s = cos128 * scale, sin128 * scale
        return y * c + pltpu.roll(y, HEAD_DIM // 2, 1) * s

    def rope64(y, scale):
        c, s = cos64 * scale, sin64 * scale
        partner = jnp.where(first64, pltpu.roll(y, V7X_LANES - IDX_DIM // 2, 1),
                            pltpu.roll(y, IDX_DIM // 2, 1))
        return y * c + partner * s

    def store_roped(out_ref, y, fn, scale):
        for h in range(GROUP_WIDTH // V7X_LANES):
            sl = slice(h * V7X_LANES, (h + 1) * V7X_LANES)
            out_ref[:, sl] = fn(y[:, sl], scale).astype(out_ref.dtype)

    qk_scale = HEAD_DIM ** -0.5
    store_roped(rq_ref, proj(0), rope128, 1.0)
    store_roped(rk_ref, proj(1), rope128, qk_scale)
    rv_ref[...] = proj(2).astype(BF16)
    rg_ref[...] = proj(3).astype(BF16)
    store_roped(aq_ref, proj(4), rope128, qk_scale * math.log2(math.e))
    store_roped(ak_ref, proj(5), rope128, 1.0)
    store_roped(iq_ref, proj(7), rope64, IDX_DIM ** -0.5)

    avT = lax.dot_general(wavT_ref[...], xb, (((1,), (1,)), ((), ())),
                          preferred_element_type=F32)
    kb = avT_ref.shape[3]
    for j in range(avT_ref.shape[1]):
        avT_ref[0, j] = avT[:, j * kb:(j + 1) * kb].astype(BF16)

    yl = jnp.dot(xb, wl_ref[...], preferred_element_type=F32)
    ik_ref[...] = rope64(yl[:, :V7X_LANES], 1.0).astype(BF16)
    iwT = yl[:, V7X_LANES:].T
    iwT_ref[0] = iwT[:IDX_HEADS, :] * (IDX_HEADS ** -0.5)


def _inproj(x2d, pos2d, freqs, w_in, layer, w_avT, w_last, batch, seq):
    m, d = x2d.shape
    tm = _tiles(seq)["proj_rows"]
    kb = _tiles(seq)["attn_keys"]
    nj = seq // tm
    row = lambda b, j: (b * nj + j, 0)
    rows_spec = lambda width: pl.BlockSpec((tm, width), row)
    out_shape = [jax.ShapeDtypeStruct((m, GROUP_WIDTH), BF16)] * 6 + [
        jax.ShapeDtypeStruct((batch, seq // kb, GROUP_WIDTH, kb), BF16),
        jax.ShapeDtypeStruct((m, GROUP_WIDTH), BF16),
        jax.ShapeDtypeStruct((m, V7X_LANES), BF16),
        jax.ShapeDtypeStruct((batch, IDX_HEADS, seq), F32),
    ]
    out_specs = [rows_spec(GROUP_WIDTH)] * 6 + [
        pl.BlockSpec((1, tm // kb, GROUP_WIDTH, kb), lambda b, j: (b, j, 0, 0)),
        rows_spec(GROUP_WIDTH),
        rows_spec(V7X_LANES),
        pl.BlockSpec((1, IDX_HEADS, tm), lambda b, j: (b, 0, j)),
    ]
    n_main = (w_in.shape[-1] // GROUP_WIDTH) * GROUP_WIDTH
    in_specs = [rows_spec(d), rows_spec(1), _resident(freqs.shape),
                _layer_resident(w_in.shape, layer, n_main), _resident(w_avT.shape), _resident(w_last.shape)]
    return pl.pallas_call(
        _inproj_kernel, grid=(batch, nj), in_specs=in_specs, out_specs=out_specs,
        out_shape=out_shape, compiler_params=_params(2), name="inproj_rope",
    )(x2d, pos2d, freqs, w_in, w_avT, w_last)


def _retention_kernel(q_ref, k_ref, v_ref, g_ref, gain_ref, o_ref):
    seq = q_ref.shape[0]
    c = RET_CHUNK
    n_chunks = seq // c
    ri = lax.broadcasted_iota(jnp.int32, (c, c), 0).astype(F32)
    ci = lax.broadcasted_iota(jnp.int32, (c, c), 1).astype(F32)
    rel = ri - ci
    pos_col = lax.broadcasted_iota(jnp.int32, (c, 1), 0).astype(F32)

    log_g = [math.log(1.0 - 2.0 ** (-5.0 - h)) for h in range(RET_HEADS)]
    decay_intra = [jnp.where(rel >= 0, jnp.exp(lg * jnp.maximum(rel, 0.0)), 0.0) for lg in log_g]
    decay_q = [jnp.exp(lg * (pos_col + 1.0)) for lg in log_g]
    decay_k = [jnp.exp(lg * (c - 1.0 - pos_col)) for lg in log_g]

    nt = (((1,), (1,)), ((), ()))
    heads = [slice(h * HEAD_DIM, (h + 1) * HEAD_DIM) for h in range(RET_HEADS)]
    chunk_decay = [math.exp(lg * c) for lg in log_g]

    def step(n, states):
        rows = [pl.ds(pl.multiple_of((n * RET_STEP_CHUNKS + i) * c, c), c)
                for i in range(RET_STEP_CHUNKS)]
        q = [[q_ref[r, sl] for r in rows] for sl in heads]
        k = [[k_ref[r, sl] for r in rows] for sl in heads]
        v = [[v_ref[r, sl] for r in rows] for sl in heads]
        scores = [[lax.dot_general(q[h][i], k[h][i], nt, preferred_element_type=F32)
                   for i in range(RET_STEP_CHUNKS)] for h in range(RET_HEADS)]
        kv = [[jnp.dot((k[h][i].astype(F32) * decay_k[h]).T.astype(BF16), v[h][i],
                       preferred_element_type=F32)
               for i in range(RET_STEP_CHUNKS)] for h in range(RET_HEADS)]
        seen = []
        for h in range(RET_HEADS):
            chain = [states[h]]
            for i in range(RET_STEP_CHUNKS):
                chain.append(chunk_decay[h] * chain[-1] + kv[h][i])
            seen.append(chain)
        weights = [[(scores[h][i] * decay_intra[h]).astype(BF16)
                    for i in range(RET_STEP_CHUNKS)] for h in range(RET_HEADS)]
        outs = [[jnp.dot(weights[h][i], v[h][i], preferred_element_type=F32)
                 + jnp.dot(q[h][i], seen[h][i].astype(BF16), preferred_element_type=F32) * decay_q[h]
                 for i in range(RET_STEP_CHUNKS)] for h in range(RET_HEADS)]
        for h, sl in enumerate(heads):
            for i, r in enumerate(rows):
                out = outs[h][i]
                mu = jnp.mean(out, axis=-1, keepdims=True)
                cen = out - mu
                var = jnp.mean(cen * cen, axis=-1, keepdims=True)
                y = cen * lax.rsqrt(var + LN_EPS) * gain_ref[:, sl]
                g = g_ref[r, sl].astype(F32)
                o_ref[r, sl] = (g * jax.nn.sigmoid(g) * y).astype(o_ref.dtype)
        return tuple(chain[-1] for chain in seen)

    assert n_chunks % RET_STEP_CHUNKS == 0
    zero = jnp.zeros((HEAD_DIM, HEAD_DIM), F32)
    lax.fori_loop(0, n_chunks // RET_STEP_CHUNKS, step, (zero,) * RET_HEADS)


def _retention(rq, rk, rv, rg, gain, layer, batch, seq):
    m = rq.shape[0]
    blk = pl.BlockSpec((seq, GROUP_WIDTH), lambda b: (b, 0))
    return pl.pallas_call(
        _retention_kernel, grid=(batch,),
        in_specs=[blk, blk, blk, blk, _layer_resident(gain.shape, layer)], out_specs=blk,
        out_shape=jax.ShapeDtypeStruct((m, GROUP_WIDTH), BF16),
        compiler_params=_params(1), name="retention_gn_gate",
    )(rq, rk, rv, rg, gain)


def _sortable_key(score):
    score = jnp.where(score == 0.0, 0.0, score)
    bits = lax.bitcast_convert_type(score, jnp.int32)
    return jnp.where(bits < 0, bits ^ jnp.int32(0x7FFFFFFF), bits)


def _attn_kernel(topk, rb, iq_ref, ik_ref, iwT_ref, aq_ref, ak_ref, avT_ref, o_ref,
                 key_ref, half_ref, acc_ref):
    tq = iq_ref.shape[0]
    kb = avT_ref.shape[3]
    qi = pl.program_id(1)
    n_blk = ((qi + 1) * tq) // kb
    n_scan = n_blk * (kb // rb)
    kf = float(topk)
    nt = (((1,), (1,)), ((), ()))

    def rows_of(j, size=kb):
        return pl.ds(pl.multiple_of(j * size, size), size)

    def index_tile(j, r0, n_rows, l0, on_diagonal):
        rows = pl.ds(pl.multiple_of(j * kb + r0, rb), n_rows)
        kk = ik_ref[rows, :]
        lane = lax.broadcasted_iota(jnp.int32, kk.shape, 1)
        k_lo = jnp.where(lane < IDX_DIM, kk, jnp.zeros_like(kk))
        k_hi = jnp.where(lane >= IDX_DIM, kk, jnp.zeros_like(kk))
        logits = []
        for pair in range(IDX_HEADS // 2):
            qp = iq_ref[l0:, pair * V7X_LANES:(pair + 1) * V7X_LANES]
            logits += [lax.dot_general(kmat, qp, nt, preferred_element_type=F32)
                       for kmat in (k_lo, k_hi)]
        score = jnp.zeros((n_rows, tq - l0), F32)
        for h in range(IDX_HEADS):
            score = score + jnp.maximum(logits[h], 0.0) * iwT_ref[0, h:h + 1, l0:]
        key = _sortable_key(score)
        if on_diagonal:
            key_pos = lax.broadcasted_iota(jnp.int32, key.shape, 0) + (j * kb + r0)
            query_pos = lax.broadcasted_iota(jnp.int32, key.shape, 1) + (qi * tq + l0)
            key = jnp.where(key_pos <= query_pos, key, jnp.int32(INT_MIN))
        key_ref[rows, l0:] = key
        half_ref[rows, l0:] = (key >> I16_BITS).astype(jnp.int16)
        if l0:
            key_ref[rows, :l0] = jnp.full((n_rows, l0), INT_MIN, jnp.int32)
            half_ref[rows, :l0] = jnp.full((n_rows, l0), I16_MIN, jnp.int16)

    def index_below(j, carry):
        index_tile(j, 0, kb, 0, on_diagonal=False)
        return carry

    assert tq == kb
    lax.fori_loop(0, n_blk - 1, index_below, 0)
    for r in range(0, kb, rb):
        index_tile(n_blk - 1, r, rb, r, on_diagonal=True)

    dead = tq - rb

    def count_half_ge(cand):
        cand16 = cand.astype(jnp.int16)

        def partial_count(values, cand_part):
            ones = jnp.where(values >= cand_part, jnp.int16(1), jnp.int16(0))
            parts = [ones[r:r + I16_TILE_ROWS, :] for r in range(0, rb, I16_TILE_ROWS)]
            while len(parts) > 1:
                parts = [a + b for a, b in zip(parts[0::2], parts[1::2])]
            return parts[0]

        assert key_ref.shape[0] // I16_TILE_ROWS <= I16_MAX

        def body(j, acc):
            return acc + partial_count(half_ref[rows_of(j, rb), :], cand16)
        acc = lax.fori_loop(0, n_scan - 1, body, jnp.zeros((I16_TILE_ROWS, tq), jnp.int16))
        last = partial_count(half_ref[rows_of(n_scan - 1, rb), dead:], cand16[:, dead:])
        acc = jnp.concatenate([acc[:, :dead], acc[:, dead:] + last], axis=1)
        return jnp.sum(acc.astype(jnp.int32).astype(F32), axis=0, keepdims=True)

    def kth_largest_half(k_wanted):
        prefix0 = jnp.where(count_half_ge(jnp.zeros((1, tq), jnp.int32)) >= k_wanted,
                            jnp.int32(0), jnp.int32(I16_MIN))

        def step(it, prefix):
            cand = prefix | jnp.left_shift(jnp.int32(1), I16_BITS - 2 - it)
            return jnp.where(count_half_ge(cand) >= k_wanted, cand, prefix)
        return lax.fori_loop(0, I16_BITS - 1, step, prefix0)

    hi = kth_largest_half(kf)
    above = jnp.where(hi == I16_MAX, 0.0, count_half_ge(jnp.minimum(hi + 1, I16_MAX)))

    hi16 = hi.astype(jnp.int16)

    def low_halves(j, carry):
        key = key_ref[rows_of(j, rb), :]
        low = ((key & ((1 << I16_BITS) - 1)) + I16_MIN).astype(jnp.int16)
        shares_hi = half_ref[rows_of(j, rb), :] == hi16
        half_ref[rows_of(j, rb), :] = jnp.where(shares_hi, low, jnp.int16(I16_MIN))
        return carry

    lax.fori_loop(0, n_scan, low_halves, 0)
    lo = kth_largest_half(kf - above)
    thr = (hi << I16_BITS) | (lo - I16_MIN)
    beyond = jnp.where(lo == I16_MAX, 0.0, count_half_ge(jnp.minimum(lo + 1, I16_MAX)))
    need = kf - above - beyond
    n_eq = count_half_ge(lo) - beyond

    overfull = jnp.where((n_eq > need) & (thr > INT_MIN), 1.0, 0.0)
    has_ties = jnp.max(overfull) > 0.0

    @pl.when(has_ties)
    def _():
        r = lax.broadcasted_iota(jnp.int32, (kb, kb), 0)
        c = lax.broadcasted_iota(jnp.int32, (kb, kb), 1)
        lower = jnp.where(r >= c, 1.0, 0.0).astype(BF16)

        def demote(j, seen):
            key = key_ref[rows_of(j), :]
            tied = key == thr
            rank = seen + jnp.dot(lower, jnp.where(tied, 1.0, 0.0).astype(BF16),
                                  preferred_element_type=F32)
            key_ref[rows_of(j), :] = jnp.where(tied & (rank > need), jnp.int32(INT_MIN), key)
            return rank[kb - 1:kb, :]

        lax.fori_loop(0, n_blk, demote, jnp.zeros((1, tq), F32))

    floor = jnp.maximum(thr, jnp.int32(INT_MIN + 1))

    def attend():
        acc_ref[...] = jnp.zeros(acc_ref.shape, F32)
        ones_rows = jnp.ones((BF16_TILE_ROWS, kb), BF16)

        def block(j, carry):
            ms, ls = carry
            bias = jnp.where(key_ref[rows_of(j), :] >= floor, 0.0, MASKED)
            heads = [slice(h * HEAD_DIM, (h + 1) * HEAD_DIM) for h in range(ATT_HEADS)]
            logits = [lax.dot_general(ak_ref[rows_of(j), sl], aq_ref[:, sl], nt,
                                      preferred_element_type=F32) for sl in heads]
            new_ms, probs, rescales = [], [], []
            for h in range(ATT_HEADS):
                s = logits[h] + bias
                m_new = jnp.maximum(ms[h], jnp.max(s, axis=0, keepdims=True))
                probs.append(jnp.exp2(s - m_new).astype(BF16))
                rescales.append(jnp.exp2(ms[h] - m_new))
                new_ms.append(m_new)
            new_ls = []
            for h, sl in enumerate(heads):
                v_aug = jnp.concatenate([avT_ref[0, j, sl, :], ones_rows], axis=0)
                pv = jnp.dot(v_aug, probs[h], preferred_element_type=F32)
                acc_ref[h] = rescales[h] * acc_ref[h] + pv[:HEAD_DIM]
                new_ls.append(rescales[h] * ls[h] + pv[HEAD_DIM:HEAD_DIM + 1])
            return tuple(new_ms), tuple(new_ls)

        init = ((jnp.full((1, tq), MASKED, F32),) * ATT_HEADS,
                (jnp.zeros((1, tq), F32),) * ATT_HEADS)
        _, ls = lax.fori_loop(0, n_blk, block, init)
        for h in range(ATT_HEADS):
            sl = slice(h * HEAD_DIM, (h + 1) * HEAD_DIM)
            o_ref[:, sl] = (acc_ref[h] / ls[h]).T.astype(o_ref.dtype)

    attend()


def _indexed_attention(iq, ik, iwT, aq, ak, avT, batch, seq):
    m = iq.shape[0]
    tq = _tiles(seq)["attn_q"]
    kb = avT.shape[3]
    nq = seq // tq
    topk = min(TOPK_MAX, seq // 4)
    q_spec = pl.BlockSpec((tq, GROUP_WIDTH), lambda b, i: (b * nq + i, 0))
    in_specs = [
        q_spec,
        pl.BlockSpec((seq, V7X_LANES), lambda b, i: (b, 0)),
        pl.BlockSpec((1, IDX_HEADS, tq), lambda b, i: (b, 0, i)),
        q_spec,
        pl.BlockSpec((seq, GROUP_WIDTH), lambda b, i: (b, 0)),
        pl.BlockSpec((1, seq // kb, GROUP_WIDTH, kb), lambda b, i: (b, 0, 0, 0)),
    ]
    scratch = [
        pltpu.VMEM((seq, tq), jnp.int32),
        pltpu.VMEM((seq, tq), jnp.int16),
        pltpu.VMEM((ATT_HEADS, HEAD_DIM, tq), F32),
    ]
    return pl.pallas_call(
        functools.partial(_attn_kernel, topk, _tiles(seq)["attn_scan_rows"]), grid=(batch, nq),
        in_specs=in_specs, out_specs=q_spec,
        out_shape=jax.ShapeDtypeStruct((m, GROUP_WIDTH), BF16),
        scratch_shapes=scratch,
        compiler_params=_params(2), name="indexer_topk_attention",
    )(iq, ik, iwT, aq, ak, avT)


def _layer_norm(y, gain, bias):
    mu = jnp.mean(y, axis=-1, keepdims=True)
    cen = y - mu
    var = jnp.mean(cen * cen, axis=-1, keepdims=True)
    return cen * lax.rsqrt(var + LN_EPS) * gain + bias


def _ffn_chunks(hidden):
    mxu_cols = 256
    assert hidden % mxu_cols == 0
    third = (hidden // mxu_cols + 2) // 3 * mxu_cols
    cuts = sorted({0, min(third, hidden), min(2 * third, hidden), hidden})
    return list(zip(cuts[:-1], cuts[1:]))


def _tail_kernel(alpha, ret_ref, att_ref, x_ref, wo_ref, g1_ref, b1_ref,
                 wgu_ref, wd_ref, g2_ref, b2_ref, o_ref):
    width = ret_ref.shape[1]
    hidden = wd_ref.shape[0]
    part = ret_ref.shape[0] // TAIL_ROW_PARTS
    halves = [slice(i * part, (i + 1) * part) for i in range(TAIL_ROW_PARTS)]

    def mixed(rows):
        mix = jnp.dot(ret_ref[rows, :], wo_ref[0:width, :], preferred_element_type=F32)
        mix = mix + jnp.dot(att_ref[rows, :], wo_ref[width:, :], preferred_element_type=F32)
        return _layer_norm(alpha * x_ref[rows, :] + mix, g1_ref[...], b1_ref[...])

    def ffn_chunk(xb, lo, hi):
        gate = jnp.dot(xb, wgu_ref[:, lo:hi], preferred_element_type=F32)
        up = jnp.dot(xb, wgu_ref[:, hidden + lo:hidden + hi], preferred_element_type=F32)
        act = (gate * jax.nn.sigmoid(gate) * up).astype(BF16)
        return jnp.dot(act, wd_ref[lo:hi, :], preferred_element_type=F32)

    xs = [mixed(rows) for rows in halves]
    xbs = [x.astype(BF16) for x in xs]
    accs = [alpha * x for x in xs]
    for lo, hi in _ffn_chunks(hidden):
        accs = [acc + ffn_chunk(xb, lo, hi) for acc, xb in zip(accs, xbs)]
    for rows, acc in zip(halves, accs):
        o_ref[rows, :] = _layer_norm(acc, g2_ref[...], b2_ref[...])


def _dense_tail(ret, att, x2d, w_out, g1, b1, w_gate_up, w_down, g2, b2, layer, alpha, seq):
    m, d = x2d.shape
    tm = _tiles(seq)["dense_rows"]
    rows = lambda width: pl.BlockSpec((tm, width), lambda i: (i, 0))
    consts = [w_out, g1, b1, w_gate_up, w_down, g2, b2]
    return pl.pallas_call(
        functools.partial(_tail_kernel, alpha), grid=(m // tm,),
        in_specs=[rows(GROUP_WIDTH), rows(GROUP_WIDTH), rows(d)]
        + [_layer_resident(c.shape, layer) for c in consts],
        out_specs=rows(d), out_shape=jax.ShapeDtypeStruct((m, d), F32),
        compiler_params=_params(1), name="outproj_ffn_deepnorm",
    )(ret, att, x2d, *consts)


def _rope_frequencies():
    def inv_freq(dim):
        return ROPE_THETA ** (-jnp.arange(0, dim, 2, dtype=F32) / dim)
    return jnp.concatenate([inv_freq(HEAD_DIM), inv_freq(IDX_DIM), inv_freq(IDX_DIM)])[None, :]


def _small_w_in_parts(w_in_l):
    g = GROUP_WIDTH
    w_avT = w_in_l[:, 6 * g:7 * g].T
    ik = w_in_l[:, 8 * g:8 * g + IDX_DIM]
    iw = w_in_l[:, 8 * g + IDX_DIM:8 * g + IDX_DIM + IDX_HEADS]
    pad = jnp.zeros((w_in_l.shape[0], V7X_LANES - IDX_HEADS), w_in_l.dtype)
    w_last = jnp.concatenate([ik, ik, iw, pad], axis=1)
    return w_avT, w_last


def kernel(x, positions, w_in, ret_gn_gain, w_out, ln_mix_gain, ln_mix_bias,
           w_gate_up, w_down, ln_ffn_gain, ln_ffn_bias):
    batch, seq, d = x.shape
    depth = w_in.shape[0]
    alpha = (2.0 * depth) ** 0.25
    freqs = _rope_frequencies()
    pos2d = positions.reshape(batch * seq, 1)
    x2d = x.reshape(batch * seq, d)
    w_in, w_out, w_gate_up, w_down = (w.astype(BF16) for w in (w_in, w_out, w_gate_up, w_down))
    per_row = lambda p: p[:, None, :]
    for l in range(depth):
        w_avT, w_last = _small_w_in_parts(w_in[l])
        rq, rk, rv, rg, aq, ak, avT, iq, ik, iwT = _inproj(
            x2d, pos2d, freqs, w_in, l, w_avT, w_last, batch, seq)
        ret = _retention(rq, rk, rv, rg, per_row(ret_gn_gain), l, batch, seq)
        att = _indexed_attention(iq, ik, iwT, aq, ak, avT, batch, seq)
        x2d = _dense_tail(ret, att, x2d, w_out, per_row(ln_mix_gain), per_row(ln_mix_bias),
                          w_gate_up, w_down, per_row(ln_ffn_gain), per_row(ln_ffn_bias), l, alpha, seq)
    return x2d.reshape(batch, seq, d)
```

```python
import functools
import math

import jax
import jax.numpy as jnp
from jax import lax
from jax.experimental import pallas as pl
from jax.experimental.pallas import tpu as pltpu

RET_HEADS = 4
ATT_HEADS = 4
HEAD_DIM = 128
IDX_HEADS = 8
IDX_DIM = 64
GROUP_WIDTH = 512
TOPK_MAX = 256
ROPE_THETA = 10000.0
RET_CHUNK = 128
RET_STEP_CHUNKS = 4
TAIL_ROW_PARTS = 2
LN_EPS = 1e-5

V7X_LANES = 128
BF16_TILE_ROWS = 16
V_AUG_ROWS = HEAD_DIM + BF16_TILE_ROWS
I16_TILE_ROWS = 16
V7X_VMEM_BYTES = 64 * 1024 * 1024
VMEM_LIMIT_BYTES = V7X_VMEM_BYTES - 8 * 1024 * 1024

F32 = jnp.float32
BF16 = jnp.bfloat16
INT_MIN = -(2 ** 31)
I16_BITS = 16
I16_MIN, I16_MAX = -(2 ** (I16_BITS - 1)), 2 ** (I16_BITS - 1) - 1
MASKED = -1e30


def _tiles(seq):
    t = {"proj_rows": min(512, seq), "attn_q": min(512, seq), "attn_keys": min(512, seq),
         "attn_scan_rows": min(256, seq), "dense_rows": min(512, seq)}
    for v in t.values():
        assert seq % v == 0
    assert t["proj_rows"] % t["attn_keys"] == 0 and t["attn_q"] % t["attn_keys"] == 0
    assert t["attn_keys"] % t["attn_scan_rows"] == 0
    return t


def _params(n_axes):
    return pltpu.CompilerParams(dimension_semantics=("arbitrary",) * n_axes,
                                vmem_limit_bytes=VMEM_LIMIT_BYTES)


def _resident(shape):
    zeros = (0,) * len(shape)
    return pl.BlockSpec(shape, lambda *_: zeros, pipeline_mode=pl.Buffered(1))


def _layer_resident(stacked_shape, layer, width=None):
    block = (None,) + tuple(stacked_shape[1:-1]) + (width or stacked_shape[-1],)
    index = (layer,) + (0,) * (len(stacked_shape) - 1)
    return pl.BlockSpec(block, lambda *_: index, pipeline_mode=pl.Buffered(1))


def _inproj_kernel(x_ref, pos_ref, freq_ref, wm_ref, wavT_ref, wl_ref,
                   rq_ref, rk_ref, rv_ref, rg_ref, aq_ref, ak_ref, avT_ref,
                   iq_ref, ik_ref, iwT_ref):
    xb = x_ref[...].astype(BF16)
    pos = pos_ref[...].astype(F32)
    lane = lax.broadcasted_iota(jnp.int32, (xb.shape[0], V7X_LANES), 1)

    ang = pos * freq_ref[...]
    cos, sin = jnp.cos(ang), jnp.sin(ang)
    low_lanes = lane < V7X_LANES // 2
    cos_swapped = pltpu.roll(cos, V7X_LANES // 2, 1)
    sin_swapped = pltpu.roll(sin, V7X_LANES // 2, 1)
    cos128 = jnp.where(low_lanes, cos, cos_swapped)
    sin128 = jnp.where(low_lanes, -sin, sin_swapped)
    first64 = (lane % IDX_DIM) < IDX_DIM // 2
    cos64 = jnp.where(low_lanes, cos_swapped, cos)
    sin64 = jnp.where(low_lanes, sin_swapped, sin)
    sin64 = jnp.where(first64, -sin64, sin64)

    def proj(group):
        c0 = group * GROUP_WIDTH
        return jnp.dot(xb, wm_ref[:, c0:c0 + GROUP_WIDTH], preferred_element_type=F32)

    def rope128(y, scale):
        c, s = cos128 * scale, sin128 * scale
        return y * c + pltpu.roll(y, HEAD_DIM // 2, 1) * s

    def rope64(y, scale):
        c, s = cos64 * scale, sin64 * scale
        partner = jnp.where(first64, pltpu.roll(y, V7X_LANES - IDX_DIM // 2, 1),
                            pltpu.roll(y, IDX_DIM // 2, 1))
        return y * c + partner * s

    def store_roped(out_ref, y, fn, scale):
        for h in range(GROUP_WIDTH // V7X_LANES):
            sl = slice(h * V7X_LANES, (h + 1) * V7X_LANES)
            out_ref[:, sl] = fn(y[:, sl], scale).astype(out_ref.dtype)

    qk_scale = HEAD_DIM ** -0.5
    store_roped(rq_ref, proj(0), rope128, 1.0)
    store_roped(rk_ref, proj(1), rope128, qk_scale)
    rv_ref[...] = proj(2).astype(BF16)
    rg_ref[...] = proj(3).astype(BF16)
    store_roped(aq_ref, proj(4), rope128, qk_scale * math.log2(math.e))
    store_roped(ak_ref, proj(5), rope128, 1.0)
    store_roped(iq_ref, proj(7), rope64, IDX_DIM ** -0.5)

    avT = lax.dot_general(wavT_ref[...], xb, (((1,), (1,)), ((), ())),
                          preferred_element_type=F32)
    kb = avT_ref.shape[3]
    for j in range(avT_ref.shape[1]):
        avT_ref[0, j] = avT[:, j * kb:(j + 1) * kb].astype(BF16)
        for h in range(ATT_HEADS):
            r0 = h * V_AUG_ROWS + HEAD_DIM
            avT_ref[0, j, r0:r0 + BF16_TILE_ROWS, :] = jnp.ones((BF16_TILE_ROWS, kb), BF16)

    yl = jnp.dot(xb, wl_ref[...], preferred_element_type=F32)
    ik_ref[...] = rope64(yl[:, :V7X_LANES], 1.0).astype(BF16)
    iwT = yl[:, V7X_LANES:].T
    iwT_ref[0] = iwT[:IDX_HEADS, :] * (IDX_HEADS ** -0.5)


def _inproj(x2d, pos2d, freqs, w_in, layer, w_avT, w_last, batch, seq):
    m, d = x2d.shape
    tm = _tiles(seq)["proj_rows"]
    kb = _tiles(seq)["attn_keys"]
    nj = seq // tm
    row = lambda b, j: (b * nj + j, 0)
    rows_spec = lambda width: pl.BlockSpec((tm, width), row)
    out_shape = [jax.ShapeDtypeStruct((m, GROUP_WIDTH), BF16)] * 6 + [
        jax.ShapeDtypeStruct((batch, seq // kb, ATT_HEADS * V_AUG_ROWS, kb), BF16),
        jax.ShapeDtypeStruct((m, GROUP_WIDTH), BF16),
        jax.ShapeDtypeStruct((m, V7X_LANES), BF16),
        jax.ShapeDtypeStruct((batch, IDX_HEADS, seq), F32),
    ]
    out_specs = [rows_spec(GROUP_WIDTH)] * 6 + [
        pl.BlockSpec((1, tm // kb, ATT_HEADS * V_AUG_ROWS, kb), lambda b, j: (b, j, 0, 0)),
        rows_spec(GROUP_WIDTH),
        rows_spec(V7X_LANES),
        pl.BlockSpec((1, IDX_HEADS, tm), lambda b, j: (b, 0, j)),
    ]
    n_main = (w_in.shape[-1] // GROUP_WIDTH) * GROUP_WIDTH
    in_specs = [rows_spec(d), rows_spec(1), _resident(freqs.shape),
                _layer_resident(w_in.shape, layer, n_main), _resident(w_avT.shape), _resident(w_last.shape)]
    return pl.pallas_call(
        _inproj_kernel, grid=(batch, nj), in_specs=in_specs, out_specs=out_specs,
        out_shape=out_shape, compiler_params=_params(2), name="inproj_rope",
    )(x2d, pos2d, freqs, w_in, w_avT, w_last)


def _retention_kernel(q_ref, k_ref, v_ref, g_ref, gain_ref, o_ref):
    seq = q_ref.shape[0]
    c = RET_CHUNK
    n_chunks = seq // c
    ri = lax.broadcasted_iota(jnp.int32, (c, c), 0).astype(F32)
    ci = lax.broadcasted_iota(jnp.int32, (c, c), 1).astype(F32)
    rel = ri - ci
    pos_col = lax.broadcasted_iota(jnp.int32, (c, 1), 0).astype(F32)

    log_g = [math.log(1.0 - 2.0 ** (-5.0 - h)) for h in range(RET_HEADS)]
    decay_intra = [jnp.where(rel >= 0, jnp.exp(lg * jnp.maximum(rel, 0.0)), 0.0) for lg in log_g]
    decay_q = [jnp.exp(lg * (pos_col + 1.0)) for lg in log_g]
    decay_k = [jnp.exp(lg * (c - 1.0 - pos_col)) for lg in log_g]

    nt = (((1,), (1,)), ((), ()))
    heads = [slice(h * HEAD_DIM, (h + 1) * HEAD_DIM) for h in range(RET_HEADS)]
    chunk_decay = [math.exp(lg * c) for lg in log_g]

    def step(n, states):
        rows = [pl.ds(pl.multiple_of((n * RET_STEP_CHUNKS + i) * c, c), c)
                for i in range(RET_STEP_CHUNKS)]
        q = [[q_ref[r, sl] for r in rows] for sl in heads]
        k = [[k_ref[r, sl] for r in rows] for sl in heads]
        v = [[v_ref[r, sl] for r in rows] for sl in heads]
        scores = [[lax.dot_general(q[h][i], k[h][i], nt, preferred_element_type=F32)
                   for i in range(RET_STEP_CHUNKS)] for h in range(RET_HEADS)]
        kv = [[jnp.dot((k[h][i].astype(F32) * decay_k[h]).T.astype(BF16), v[h][i],
                       preferred_element_type=F32)
               for i in range(RET_STEP_CHUNKS)] for h in range(RET_HEADS)]
        seen = []
        for h in range(RET_HEADS):
            chain = [states[h]]
            for i in range(RET_STEP_CHUNKS):
                chain.append(chunk_decay[h] * chain[-1] + kv[h][i])
            seen.append(chain)
        weights = [[(scores[h][i] * decay_intra[h]).astype(BF16)
                    for i in range(RET_STEP_CHUNKS)] for h in range(RET_HEADS)]
        outs = [[jnp.dot(weights[h][i], v[h][i], preferred_element_type=F32)
                 + jnp.dot(q[h][i], seen[h][i].astype(BF16), preferred_element_type=F32) * decay_q[h]
                 for i in range(RET_STEP_CHUNKS)] for h in range(RET_HEADS)]
        for h, sl in enumerate(heads):
            for i, r in enumerate(rows):
                out = outs[h][i]
                mu = jnp.mean(out, axis=-1, keepdims=True)
                cen = out - mu
                var = jnp.mean(cen * cen, axis=-1, keepdims=True)
                y = cen * lax.rsqrt(var + LN_EPS) * gain_ref[:, sl]
                g = g_ref[r, sl].astype(F32)
                o_ref[r, sl] = (g * jax.nn.sigmoid(g) * y).astype(o_ref.dtype)
        return tuple(chain[-1] for chain in seen)

    assert n_chunks % RET_STEP_CHUNKS == 0
    zero = jnp.zeros((HEAD_DIM, HEAD_DIM), F32)
    lax.fori_loop(0, n_chunks // RET_STEP_CHUNKS, step, (zero,) * RET_HEADS)


def _retention(rq, rk, rv, rg, gain, layer, batch, seq):
    m = rq.shape[0]
    blk = pl.BlockSpec((seq, GROUP_WIDTH), lambda b: (b, 0))
    return pl.pallas_call(
        _retention_kernel, grid=(batch,),
        in_specs=[blk, blk, blk, blk, _layer_resident(gain.shape, layer)], out_specs=blk,
        out_shape=jax.ShapeDtypeStruct((m, GROUP_WIDTH), BF16),
        compiler_params=_params(1), name="retention_gn_gate",
    )(rq, rk, rv, rg, gain)


def _sortable_key(score):
    score = jnp.where(score == 0.0, 0.0, score)
    bits = lax.bitcast_convert_type(score, jnp.int32)
    return jnp.where(bits < 0, bits ^ jnp.int32(0x7FFFFFFF), bits)


def _attn_kernel(topk, rb, iq_ref, ik_ref, iwT_ref, aq_ref, ak_ref, avT_ref, o_ref,
                 key_ref, half_ref, acc_ref):
    tq = iq_ref.shape[0]
    kb = avT_ref.shape[3]
    qi = pl.program_id(1)
    n_blk = ((qi + 1) * tq) // kb
    n_scan = n_blk * (kb // rb)
    kf = float(topk)
    nt = (((1,), (1,)), ((), ()))

    def rows_of(j, size=kb):
        return pl.ds(pl.multiple_of(j * size, size), size)

    def index_tile(j, r0, n_rows, l0, on_diagonal):
        rows = pl.ds(pl.multiple_of(j * kb + r0, rb), n_rows)
        kk = ik_ref[rows, :]
        lane = lax.broadcasted_iota(jnp.int32, kk.shape, 1)
        k_lo = jnp.where(lane < IDX_DIM, kk, jnp.zeros_like(kk))
        k_hi = jnp.where(lane >= IDX_DIM, kk, jnp.zeros_like(kk))
        logits = []
        for pair in range(IDX_HEADS // 2):
            qp = iq_ref[l0:, pair * V7X_LANES:(pair + 1) * V7X_LANES]
            logits += [lax.dot_general(kmat, qp, nt, preferred_element_type=F32)
                       for kmat in (k_lo, k_hi)]
        score = jnp.zeros((n_rows, tq - l0), F32)
        for h in range(IDX_HEADS):
            score = score + jnp.maximum(logits[h], 0.0) * iwT_ref[0, h:h + 1, l0:]
        key = _sortable_key(score)
        if on_diagonal:
            key_pos = lax.broadcasted_iota(jnp.int32, key.shape, 0) + (j * kb + r0)
            query_pos = lax.broadcasted_iota(jnp.int32, key.shape, 1) + (qi * tq + l0)
            key = jnp.where(key_pos <= query_pos, key, jnp.int32(INT_MIN))
        key_ref[rows, l0:] = key
        half_ref[rows, l0:] = (key >> I16_BITS).astype(jnp.int16)
        if l0:
            key_ref[rows, :l0] = jnp.full((n_rows, l0), INT_MIN, jnp.int32)
            half_ref[rows, :l0] = jnp.full((n_rows, l0), I16_MIN, jnp.int16)

    def index_below(j, carry):
        index_tile(j, 0, kb, 0, on_diagonal=False)
        return carry

    assert tq == kb
    lax.fori_loop(0, n_blk - 1, index_below, 0)
    for r in range(0, kb, rb):
        index_tile(n_blk - 1, r, rb, r, on_diagonal=True)

    dead = tq - rb

    def count_half_ge(cand):
        cand16 = cand.astype(jnp.int16)

        def partial_count(values, cand_part):
            ones = jnp.where(values >= cand_part, jnp.int16(1), jnp.int16(0))
            parts = [ones[r:r + I16_TILE_ROWS, :] for r in range(0, rb, I16_TILE_ROWS)]
            while len(parts) > 1:
                parts = [a + b for a, b in zip(parts[0::2], parts[1::2])]
            return parts[0]

        assert key_ref.shape[0] // I16_TILE_ROWS <= I16_MAX

        def body(j, acc):
            return acc + partial_count(half_ref[rows_of(j, rb), :], cand16)
        acc = lax.fori_loop(0, n_scan - 1, body, jnp.zeros((I16_TILE_ROWS, tq), jnp.int16))
        last = partial_count(half_ref[rows_of(n_scan - 1, rb), dead:], cand16[:, dead:])
        acc = jnp.concatenate([acc[:, :dead], acc[:, dead:] + last], axis=1)
        return jnp.sum(acc.astype(jnp.int32).astype(F32), axis=0, keepdims=True)

    def kth_largest_half(k_wanted):
        prefix0 = jnp.where(count_half_ge(jnp.zeros((1, tq), jnp.int32)) >= k_wanted,
                            jnp.int32(0), jnp.int32(I16_MIN))

        def step(it, prefix):
            cand = prefix | jnp.left_shift(jnp.int32(1), I16_BITS - 2 - it)
            return jnp.where(count_half_ge(cand) >= k_wanted, cand, prefix)
        return lax.fori_loop(0, I16_BITS - 1, step, prefix0)

    hi = kth_largest_half(kf)
    above = jnp.where(hi == I16_MAX, 0.0, count_half_ge(jnp.minimum(hi + 1, I16_MAX)))

    hi16 = hi.astype(jnp.int16)

    def low_halves(j, carry):
        key = key_ref[rows_of(j, rb), :]
        low = ((key & ((1 << I16_BITS) - 1)) + I16_MIN).astype(jnp.int16)
        shares_hi = half_ref[rows_of(j, rb), :] == hi16
        half_ref[rows_of(j, rb), :] = jnp.where(shares_hi, low, jnp.int16(I16_MIN))
        return carry

    lax.fori_loop(0, n_scan, low_halves, 0)
    lo = kth_largest_half(kf - above)
    thr = (hi << I16_BITS) | (lo - I16_MIN)
    beyond = jnp.where(lo == I16_MAX, 0.0, count_half_ge(jnp.minimum(lo + 1, I16_MAX)))
    need = kf - above - beyond
    n_eq = count_half_ge(lo) - beyond

    overfull = jnp.where((n_eq > need) & (thr > INT_MIN), 1.0, 0.0)
    has_ties = jnp.max(overfull) > 0.0

    @pl.when(has_ties)
    def _():
        r = lax.broadcasted_iota(jnp.int32, (kb, kb), 0)
        c = lax.broadcasted_iota(jnp.int32, (kb, kb), 1)
        lower = jnp.where(r >= c, 1.0, 0.0).astype(BF16)

        def demote(j, seen):
            key = key_ref[rows_of(j), :]
            tied = key == thr
            rank = seen + jnp.dot(lower, jnp.where(tied, 1.0, 0.0).astype(BF16),
                                  preferred_element_type=F32)
            key_ref[rows_of(j), :] = jnp.where(tied & (rank > need), jnp.int32(INT_MIN), key)
            return rank[kb - 1:kb, :]

        lax.fori_loop(0, n_blk, demote, jnp.zeros((1, tq), F32))

    floor = jnp.maximum(thr, jnp.int32(INT_MIN + 1))

    def attend():
        acc_ref[...] = jnp.zeros(acc_ref.shape, F32)

        def block(j, carry):
            ms, ls = carry
            bias = jnp.where(key_ref[rows_of(j), :] >= floor, 0.0, MASKED)
            heads = [slice(h * HEAD_DIM, (h + 1) * HEAD_DIM) for h in range(ATT_HEADS)]
            logits = [lax.dot_general(ak_ref[rows_of(j), sl], aq_ref[:, sl], nt,
                                      preferred_element_type=F32) for sl in heads]
            new_ms, probs, rescales = [], [], []
            for h in range(ATT_HEADS):
                s = logits[h] + bias
                m_new = jnp.maximum(ms[h], jnp.max(s, axis=0, keepdims=True))
                probs.append(jnp.exp2(s - m_new).astype(BF16))
                rescales.append(jnp.exp2(ms[h] - m_new))
                new_ms.append(m_new)
            new_ls = []
            for h, sl in enumerate(heads):
                v_aug = avT_ref[0, j, h * V_AUG_ROWS:(h + 1) * V_AUG_ROWS, :]
                pv = jnp.dot(v_aug, probs[h], preferred_element_type=F32)
                acc_ref[h] = rescales[h] * acc_ref[h] + pv[:HEAD_DIM]
                new_ls.append(rescales[h] * ls[h] + pv[HEAD_DIM:HEAD_DIM + 1])
            return tuple(new_ms), tuple(new_ls)

        init = ((jnp.full((1, tq), MASKED, F32),) * ATT_HEADS,
                (jnp.zeros((1, tq), F32),) * ATT_HEADS)
        _, ls = lax.fori_loop(0, n_blk, block, init)
        for h in range(ATT_HEADS):
            sl = slice(h * HEAD_DIM, (h + 1) * HEAD_DIM)
            o_ref[:, sl] = (acc_ref[h] / ls[h]).T.astype(o_ref.dtype)

    attend()


def _indexed_attention(iq, ik, iwT, aq, ak, avT, batch, seq):
    m = iq.shape[0]
    tq = _tiles(seq)["attn_q"]
    kb = avT.shape[3]
    nq = seq // tq
    topk = min(TOPK_MAX, seq // 4)
    q_spec = pl.BlockSpec((tq, GROUP_WIDTH), lambda b, i: (b * nq + i, 0))
    in_specs = [
        q_spec,
        pl.BlockSpec((seq, V7X_LANES), lambda b, i: (b, 0)),
        pl.BlockSpec((1, IDX_HEADS, tq), lambda b, i: (b, 0, i)),
        q_spec,
        pl.BlockSpec((seq, GROUP_WIDTH), lambda b, i: (b, 0)),
        pl.BlockSpec((1, seq // kb, ATT_HEADS * V_AUG_ROWS, kb), lambda b, i: (b, 0, 0, 0)),
    ]
    scratch = [
        pltpu.VMEM((seq, tq), jnp.int32),
        pltpu.VMEM((seq, tq), jnp.int16),
        pltpu.VMEM((ATT_HEADS, HEAD_DIM, tq), F32),
    ]
    return pl.pallas_call(
        functools.partial(_attn_kernel, topk, _tiles(seq)["attn_scan_rows"]), grid=(batch, nq),
        in_specs=in_specs, out_specs=q_spec,
        out_shape=jax.ShapeDtypeStruct((m, GROUP_WIDTH), BF16),
        scratch_shapes=scratch,
        compiler_params=_params(2), name="indexer_topk_attention",
    )(iq, ik, iwT, aq, ak, avT)


def _layer_norm(y, gain, bias):
    mu = jnp.mean(y, axis=-1, keepdims=True)
    cen = y - mu
    var = jnp.mean(cen * cen, axis=-1, keepdims=True)
    return cen * lax.rsqrt(var + LN_EPS) * gain + bias


def _ffn_chunks(hidden):
    mxu_cols = 256
    assert hidden % mxu_cols == 0
    third = (hidden // mxu_cols + 2) // 3 * mxu_cols
    cuts = sorted({0, min(third, hidden), min(2 * third, hidden), hidden})
    return list(zip(cuts[:-1], cuts[1:]))


def _tail_kernel(alpha, ret_ref, att_ref, x_ref, wo_ref, g1_ref, b1_ref,
                 wgu_ref, wd_ref, g2_ref, b2_ref, o_ref):
    width = ret_ref.shape[1]
    hidden = wd_ref.shape[0]
    part = ret_ref.shape[0] // TAIL_ROW_PARTS
    halves = [slice(i * part, (i + 1) * part) for i in range(TAIL_ROW_PARTS)]

    def mixed(rows):
        mix = jnp.dot(ret_ref[rows, :], wo_ref[0:width, :], preferred_element_type=F32)
        mix = mix + jnp.dot(att_ref[rows, :], wo_ref[width:, :], preferred_element_type=F32)
        return _layer_norm(alpha * x_ref[rows, :] + mix, g1_ref[...], b1_ref[...])

    def ffn_chunk(xb, lo, hi):
        gate = jnp.dot(xb, wgu_ref[:, lo:hi], preferred_element_type=F32)
        up = jnp.dot(xb, wgu_ref[:, hidden + lo:hidden + hi], preferred_element_type=F32)
        act = (gate * jax.nn.sigmoid(gate) * up).astype(BF16)
        return jnp.dot(act, wd_ref[lo:hi, :], preferred_element_type=F32)

    xs = [mixed(rows) for rows in halves]
    xbs = [x.astype(BF16) for x in xs]
    accs = [alpha * x for x in xs]
    for lo, hi in _ffn_chunks(hidden):
        accs = [acc + ffn_chunk(xb, lo, hi) for acc, xb in zip(accs, xbs)]
    for rows, acc in zip(halves, accs):
        o_ref[rows, :] = _layer_norm(acc, g2_ref[...], b2_ref[...])


def _dense_tail(ret, att, x2d, w_out, g1, b1, w_gate_up, w_down, g2, b2, layer, alpha, seq):
    m, d = x2d.shape
    tm = _tiles(seq)["dense_rows"]
    rows = lambda width: pl.BlockSpec((tm, width), lambda i: (i, 0))
    consts = [w_out, g1, b1, w_gate_up, w_down, g2, b2]
    return pl.pallas_call(
        functools.partial(_tail_kernel, alpha), grid=(m // tm,),
        in_specs=[rows(GROUP_WIDTH), rows(GROUP_WIDTH), rows(d)]
        + [_layer_resident(c.shape, layer) for c in consts],
        out_specs=rows(d), out_shape=jax.ShapeDtypeStruct((m, d), F32),
        compiler_params=_params(1), name="outproj_ffn_deepnorm",
    )(ret, att, x2d, *consts)


def _rope_frequencies():
    def inv_freq(dim):
        return ROPE_THETA ** (-jnp.arange(0, dim, 2, dtype=F32) / dim)
    return jnp.concatenate([inv_freq(HEAD_DIM), inv_freq(IDX_DIM), inv_freq(IDX_DIM)])[None, :]


def _small_w_in_parts(w_in_l):
    g = GROUP_WIDTH
    w_v = w_in_l[:, 6 * g:7 * g].T.reshape(ATT_HEADS, HEAD_DIM, -1)
    w_avT = jnp.pad(w_v, ((0, 0), (0, BF16_TILE_ROWS), (0, 0))).reshape(ATT_HEADS * V_AUG_ROWS, -1)
    ik = w_in_l[:, 8 * g:8 * g + IDX_DIM]
    iw = w_in_l[:, 8 * g + IDX_DIM:8 * g + IDX_DIM + IDX_HEADS]
    pad = jnp.zeros((w_in_l.shape[0], V7X_LANES - IDX_HEADS), w_in_l.dtype)
    w_last = jnp.concatenate([ik, ik, iw, pad], axis=1)
    return w_avT, w_last


def kernel(x, positions, w_in, ret_gn_gain, w_out, ln_mix_gain, ln_mix_bias,
           w_gate_up, w_down, ln_ffn_gain, ln_ffn_bias):
    batch, seq, d = x.shape
    depth = w_in.shape[0]
    alpha = (2.0 * depth) ** 0.25
    freqs = _rope_frequencies()
    pos2d = positions.reshape(batch * seq, 1)
    x2d = x.reshape(batch * seq, d)
    w_in, w_out, w_gate_up, w_down = (w.astype(BF16) for w in (w_in, w_out, w_gate_up, w_down))
    per_row = lambda p: p[:, None, :]
    for l in range(depth):
        w_avT, w_last = _small_w_in_parts(w_in[l])
        rq, rk, rv, rg, aq, ak, avT, iq, ik, iwT = _inproj(
            x2d, pos2d, freqs, w_in, l, w_avT, w_last, batch, seq)
        ret = _retention(rq, rk, rv, rg, per_row(ret_gn_gain), l, batch, seq)
        att = _indexed_attention(iq, ik, iwT, aq, ak, avT, batch, seq)
        x2d = _dense_tail(ret, att, x2d, w_out, per_row(ln_mix_gain), per_row(ln_mix_bias),
                          w_gate_up, w_down, per_row(ln_ffn_gain), per_row(ln_ffn_bias), l, alpha, seq)
    return x2d.reshape(batch, seq, d)
```
